```python
import math
import jax, jax.numpy as jnp
from jax import lax
import numpy as np

D_MODEL = 2048
BATCH = 8
SEQ = 4096
DEPTH = 1

D_MIX = D_MODEL
D_ATTN = D_MIX // 2
D_CONV = D_MIX - D_ATTN
ATTN_HEAD_DIM = 64
ATTN_VALUE_DIM = 2 * ATTN_HEAD_DIM
N_ATTN_HEADS = D_ATTN // ATTN_VALUE_DIM
QK_COLS = N_ATTN_HEADS * 2 * ATTN_HEAD_DIM
V_COLS = N_ATTN_HEADS * ATTN_VALUE_DIM
GLU_COLS = 2 * D_CONV
D_IN_PROJ = 2 * QK_COLS + V_COLS + GLU_COLS
CONV_WIDTH = 31
QUERY_BLOCK = 128
REL_BUCKETS = 32
REL_MAX_EXACT = REL_BUCKETS // 2
REL_MAX_DISTANCE = 128
PEER_HEADS = 8
PEER_N_KEYS = 128
PEER_N_EXPERTS = PEER_N_KEYS * PEER_N_KEYS
PEER_KEY_DIM = 256
PEER_HALF_DIM = PEER_KEY_DIM // 2
PEER_TOPK = 16
PEER_TOKEN_BLOCK = 128
LN_EPS = 1e-5
DEEPNORM_ALPHA = (2.0 * DEPTH) ** 0.25
DEEPNORM_BETA = (8.0 * DEPTH) ** -0.25

kernel_name = "hybrid_diffattn_conformer_peer_deepnorm"


def _layer_norm(x, g, b):
    xf = x.astype(jnp.float32)
    mu = jnp.mean(xf, axis=-1, keepdims=True)
    var = jnp.mean(jnp.square(xf - mu), axis=-1, keepdims=True)
    y = (xf - mu) * lax.rsqrt(var + LN_EPS)
    return (y * g.astype(jnp.float32) + b.astype(jnp.float32)).astype(x.dtype)


def _rms_norm(x, g):
    xf = x.astype(jnp.float32)
    y = xf * lax.rsqrt(jnp.mean(jnp.square(xf), axis=-1, keepdims=True) + LN_EPS)
    return (y * g.astype(jnp.float32)).astype(x.dtype)


def _t5_bucket(dist):
    n_f = jnp.maximum(dist, 1).astype(jnp.float32)
    large = REL_MAX_EXACT + (
        jnp.log(n_f / REL_MAX_EXACT) / math.log(REL_MAX_DISTANCE / REL_MAX_EXACT)
        * (REL_BUCKETS - REL_MAX_EXACT)).astype(jnp.int32)
    large = jnp.minimum(large, REL_BUCKETS - 1)
    return jnp.where(dist < REL_MAX_EXACT, dist, large)


def _diff_attention(q, k, v, lam, rel_bias):
    B, S = q.shape[0], q.shape[1]
    nb = S // QUERY_BLOCK
    q_blocks = q.reshape(B, nb, QUERY_BLOCK, N_ATTN_HEADS, 2, ATTN_HEAD_DIM).transpose(1, 0, 3, 4, 2, 5)
    kt = k.transpose(0, 2, 3, 1, 4)
    vt = v.transpose(0, 2, 1, 3)
    key_pos = jnp.arange(S, dtype=jnp.int32)
    scale = ATTN_HEAD_DIM ** -0.5

    def block(args):
        q_blk, start = args
        q_pos = start + jnp.arange(QUERY_BLOCK, dtype=jnp.int32)
        dist = q_pos[:, None] - key_pos[None, :]
        bias = rel_bias[_t5_bucket(jnp.maximum(dist, 0))].transpose(2, 0, 1)
        logits = jnp.einsum('bhcqd,bhckd->bhcqk', q_blk, kt).astype(jnp.float32) * scale
        logits = logits + bias[None, :, None].astype(jnp.float32)
        logits = jnp.where((dist >= 0)[None, None, None], logits, -jnp.inf)
        p = jax.nn.softmax(logits, axis=-1)
        p = (p[:, :, 0] - lam * p[:, :, 1]).astype(vt.dtype)
        return jnp.einsum('bhqk,bhkd->bqhd', p, vt)

    starts = jnp.arange(nb, dtype=jnp.int32) * QUERY_BLOCK
    out = lax.map(block, (q_blocks, starts))
    return out.transpose(1, 0, 2, 3, 4).reshape(B, S, N_ATTN_HEADS, ATTN_VALUE_DIM)


def _conformer_conv(glu_in, b_glu, conv_w, conv_b, ln_g, ln_b):
    h = glu_in + b_glu
    a, gate = jnp.split(h, 2, axis=-1)
    h = a * jax.nn.sigmoid(gate)
    h = lax.conv_general_dilated(
        h, conv_w.reshape(CONV_WIDTH, 1, D_CONV).astype(h.dtype),
        window_strides=(1,), padding=[(CONV_WIDTH - 1, 0)],
        dimension_numbers=('NWC', 'WIO', 'NWC'), feature_group_count=D_CONV) + conv_b
    return jax.nn.silu(_layer_norm(h, ln_g, ln_b))


def _peer(x, wq, sub_keys, u_table, v_table):
    B, S, D = x.shape
    tokens = x.reshape(-1, PEER_TOKEN_BLOCK, D)

    def chunk(xc):
        T = xc.shape[0]
        q = (xc @ wq).reshape(T, PEER_HEADS, 2, PEER_HALF_DIM)
        s = jnp.einsum('thcd,hcnd->thcn', q, sub_keys)
        s1, i1 = lax.top_k(s[:, :, 0], PEER_TOPK)
        s2, i2 = lax.top_k(s[:, :, 1], PEER_TOPK)
        cand = (s1[..., :, None] + s2[..., None, :]).reshape(T, PEER_HEADS, PEER_TOPK * PEER_TOPK)
        sc, ci = lax.top_k(cand, PEER_TOPK)
        e = (jnp.take_along_axis(i1, ci // PEER_TOPK, axis=-1) * PEER_N_KEYS
             + jnp.take_along_axis(i2, ci % PEER_TOPK, axis=-1))
        g = jax.nn.softmax(sc.astype(jnp.float32), axis=-1).astype(xc.dtype)
        u = u_table[e]
        act = jax.nn.gelu(jnp.einsum('td,thkd->thk', xc, u), approximate=False)
        return jnp.einsum('thk,thkd->td', g * act, v_table[e])

    return lax.map(chunk, tokens).reshape(B, S, D)


def setup_inputs(seed: int = 0) -> dict:
    key = jax.random.key(seed)
    ks = jax.random.split(key, 24)
    f32 = jnp.float32
    nrm = lambda k, shape: jax.random.normal(k, shape, f32)
    L = DEPTH
    return {
        "x": nrm(ks[0], (BATCH, SEQ, D_MODEL)),
        "w_in": nrm(ks[1], (L, D_MODEL, D_IN_PROJ)) * D_MODEL ** -0.5,
        "b_glu": 0.02 * nrm(ks[2], (L, GLU_COLS)),
        "conv_w": nrm(ks[3], (L, CONV_WIDTH, D_CONV)) * CONV_WIDTH ** -0.5,
        "conv_b": 0.02 * nrm(ks[4], (L, D_CONV)),
        "conv_ln_g": 1.0 + 0.02 * nrm(ks[5], (L, D_CONV)),
        "conv_ln_b": 0.02 * nrm(ks[6], (L, D_CONV)),
        "lambda_q1": 0.1 * nrm(ks[7], (L, ATTN_HEAD_DIM)),
        "lambda_k1": 0.1 * nrm(ks[8], (L, ATTN_HEAD_DIM)),
        "lambda_q2": 0.1 * nrm(ks[9], (L, ATTN_HEAD_DIM)),
        "lambda_k2": 0.1 * nrm(ks[10], (L, ATTN_HEAD_DIM)),
        "subln_g": 1.0 + 0.02 * nrm(ks[11], (L, ATTN_VALUE_DIM)),
        "rel_bias": 0.1 * nrm(ks[12], (REL_BUCKETS, N_ATTN_HEADS)),
        "w_out": nrm(ks[13], (L, D_MIX, D_MODEL)) * (D_MIX ** -0.5) * DEEPNORM_BETA,
        "ln1_g": 1.0 + 0.02 * nrm(ks[14], (L, D_MODEL)),
        "ln1_b": 0.02 * nrm(ks[15], (L, D_MODEL)),
        "peer_wq": nrm(ks[16], (L, D_MODEL, PEER_HEADS * PEER_KEY_DIM)) * D_MODEL ** -0.5,
        "peer_keys": nrm(ks[17], (L, PEER_HEADS, 2, PEER_N_KEYS, PEER_HALF_DIM)) * PEER_HALF_DIM ** -0.5,
        "peer_u": nrm(ks[18], (L, PEER_N_EXPERTS, D_MODEL)) * D_MODEL ** -0.5,
        "peer_v": nrm(ks[19], (L, PEER_N_EXPERTS, D_MODEL)) * DEEPNORM_BETA,
        "ln2_g": 1.0 + 0.02 * nrm(ks[20], (L, D_MODEL)),
        "ln2_b": 0.02 * nrm(ks[21], (L, D_MODEL)),
    }


def reference(x, w_in, b_glu, conv_w, conv_b, conv_ln_g, conv_ln_b,
              lambda_q1, lambda_k1, lambda_q2, lambda_k2, subln_g, rel_bias,
              w_out, ln1_g, ln1_b, peer_wq, peer_keys, peer_u, peer_v,
              ln2_g, ln2_b):
    B, S, _ = x.shape
    for layer in range(DEPTH):
        proj = jnp.einsum('bsd,de->bse', x, w_in[layer])
        q = proj[..., :QK_COLS].reshape(B, S, N_ATTN_HEADS, 2, ATTN_HEAD_DIM)
        k = proj[..., QK_COLS:2 * QK_COLS].reshape(B, S, N_ATTN_HEADS, 2, ATTN_HEAD_DIM)
        v = proj[..., 2 * QK_COLS:2 * QK_COLS + V_COLS].reshape(B, S, N_ATTN_HEADS, ATTN_VALUE_DIM)
        glu_in = proj[..., 2 * QK_COLS + V_COLS:]

        lambda_init = 0.8 - 0.6 * math.exp(-0.3 * layer)
        lam = (jnp.exp(jnp.sum(lambda_q1[layer].astype(jnp.float32) * lambda_k1[layer].astype(jnp.float32)))
               - jnp.exp(jnp.sum(lambda_q2[layer].astype(jnp.float32) * lambda_k2[layer].astype(jnp.float32)))
               + lambda_init)
        attn = _diff_attention(q, k, v, lam, rel_bias)
        attn = (_rms_norm(attn, subln_g[layer]) * (1.0 - lambda_init)).reshape(B, S, D_ATTN)

        conv = _conformer_conv(glu_in, b_glu[layer], conv_w[layer], conv_b[layer],
                               conv_ln_g[layer], conv_ln_b[layer])

        mix = jnp.einsum('bse,ed->bsd', jnp.concatenate([attn, conv], axis=-1), w_out[layer])
        x = _layer_norm(DEEPNORM_ALPHA * x + mix, ln1_g[layer], ln1_b[layer])

        ffn = _peer(x, peer_wq[layer], peer_keys[layer], peer_u[layer], peer_v[layer])
        x = _layer_norm(DEEPNORM_ALPHA * x + ffn, ln2_g[layer], ln2_b[layer])
    return x
```

```python
import functools
import math

import jax
import jax.numpy as jnp
from jax import lax
from jax.experimental import pallas as pl
from jax.experimental.pallas import tpu as pltpu

F32 = jnp.float32
BF16 = jnp.bfloat16

ATTN_HEAD_DIM = 64
ATTN_VALUE_DIM = 2 * ATTN_HEAD_DIM
REL_MAX_DISTANCE = 128
PEER_TOPK = 16
LN_EPS = 1e-5
DEPTH = 1
DEEPNORM_ALPHA = (2.0 * DEPTH) ** 0.25

LANES = 128
VMEM_LIMIT_BYTES = 56 * 1024 * 1024


def _pick(limit, total, quantum=128):
    if total <= limit:
        return total
    t = (limit // quantum) * quantum
    while t > quantum and total % t:
        t -= quantum
    assert total % t == 0, (limit, total)
    return t


def _cparams(sem):
    return pltpu.CompilerParams(dimension_semantics=sem, vmem_limit_bytes=VMEM_LIMIT_BYTES)


def _ln_rows(y, g, b):
    mu = jnp.mean(y, axis=-1, keepdims=True)
    d = y - mu
    var = jnp.mean(d * d, axis=-1, keepdims=True)
    return d * lax.rsqrt(var + LN_EPS) * g + b


def _inproj_kernel(x_ref, w_ref, o_ref, xb_ref):
    @pl.when(pl.program_id(1) == 0)
    def _():
        xb_ref[...] = x_ref[...].astype(BF16)

    o_ref[...] = jnp.dot(xb_ref[...], w_ref[...], preferred_element_type=F32).astype(o_ref.dtype)


def _in_proj(x2, w):
    m, d = x2.shape
    n = w.shape[1]
    tm = _pick(1024, m)
    tn = _pick(512, n)
    return pl.pallas_call(
        _inproj_kernel,
        grid=(m // tm, n // tn),
        in_specs=[pl.BlockSpec((tm, d), lambda i, j: (i, 0)),
                  pl.BlockSpec((d, tn), lambda i, j: (0, j))],
        out_specs=pl.BlockSpec((tm, tn), lambda i, j: (i, j)),
        out_shape=jax.ShapeDtypeStruct((m, n), BF16),
        scratch_shapes=[pltpu.VMEM((tm, d), BF16)],
        compiler_params=_cparams(("parallel", "arbitrary")),
        name="in_proj",
    )(x2, w)


def _t5_bucket(dist, num_buckets):
    max_exact = num_buckets // 2
    n_f = jnp.maximum(dist, 1).astype(F32)
    large = max_exact + (
        jnp.log(n_f / max_exact) / math.log(REL_MAX_DISTANCE / max_exact)
        * (num_buckets - max_exact)).astype(jnp.int32)
    large = jnp.minimum(large, num_buckets - 1)
    return jnp.where(dist < max_exact, dist, large)


def _bias_tiles(rel_bias, t):
    nb = rel_bias.shape[0]
    assert t + 1 >= REL_MAX_DISTANCE
    r = jnp.arange(t, dtype=jnp.int32)[:, None]
    c = jnp.arange(t, dtype=jnp.int32)[None, :]
    far = rel_bias[nb - 1].astype(F32)
    d_diag = r - c
    diag = rel_bias[_t5_bucket(jnp.maximum(d_diag, 0), nb)].astype(F32) - far
    diag = jnp.where((d_diag >= 0)[..., None], diag, -jnp.inf)
    prev = rel_bias[_t5_bucket(d_diag + t, nb)].astype(F32) - far
    tiles = jnp.stack([prev, diag], 0).transpose(3, 0, 1, 2)
    return jnp.concatenate([tiles, tiles], axis=2)


def _attn_kernel(lam_ref, g_ref, q_ref, k_ref, v_ref, bias_ref, o_ref, m_ref, l_ref, acc_ref,
                 *, t, lambda_init):
    qi = pl.program_id(2)
    q = q_ref[0]
    lane = lax.broadcasted_iota(jnp.int32, q.shape, 1)
    zero = jnp.zeros_like(q)
    qs = jnp.concatenate([jnp.where(lane < ATTN_HEAD_DIM, q, zero),
                          jnp.where(lane >= ATTN_HEAD_DIM, q, zero)], axis=0)
    qs = (qs.astype(F32) * (ATTN_HEAD_DIM ** -0.5)).astype(BF16)

    m_ref[...] = jnp.full(m_ref.shape, -jnp.inf, F32)
    l_ref[...] = jnp.zeros(l_ref.shape, F32)
    acc_ref[...] = jnp.zeros(acc_ref.shape, F32)

    def step(j, bias):
        r0 = pl.multiple_of(j * t, t)
        kj = k_ref[0, pl.ds(r0, t), :]
        vj = v_ref[0, pl.ds(r0, t), :]
        s = lax.dot_general(qs, kj, (((1,), (1,)), ((), ())), preferred_element_type=F32)
        if bias is not None:
            s = s + bias
        m_prev = m_ref[...]
        m_new = jnp.maximum(m_prev, jnp.max(s, axis=1, keepdims=True))
        alpha = jnp.exp(m_prev - m_new)
        p = jnp.exp(s - m_new)
        l_ref[...] = alpha * l_ref[...] + jnp.sum(p, axis=1, keepdims=True)
        acc_ref[...] = alpha * acc_ref[...] + jnp.dot(p.astype(BF16), vj, preferred_element_type=F32)
        m_ref[...] = m_new

    def far_body(j, carry):
        step(j, None)
        return carry

    lax.fori_loop(0, jnp.maximum(qi - 1, 0), far_body, 0)

    @pl.when(qi >= 1)
    def _():
        step(qi - 1, bias_ref[0, 0])

    step(qi, bias_ref[0, 1])

    lam_p = lam_ref[...]
    lam = (jnp.exp(jnp.sum(lam_p[0:1] * lam_p[1:2], axis=1, keepdims=True))
           - jnp.exp(jnp.sum(lam_p[2:3] * lam_p[3:4], axis=1, keepdims=True)) + lambda_init)
    o = acc_ref[...] / l_ref[...]
    o = o[:t] - lam * o[t:]
    y = o * lax.rsqrt(jnp.mean(o * o, axis=-1, keepdims=True) + LN_EPS) * g_ref[...]
    o_ref[0] = (y * (1.0 - lambda_init)).astype(o_ref.dtype)


def _attention(proj3, lam_params, subln_g, bias, n_heads, lambda_init):
    b, s, _ = proj3.shape
    t = bias.shape[-1]
    dv = ATTN_VALUE_DIM
    kern = functools.partial(_attn_kernel, t=t, lambda_init=lambda_init)
    return pl.pallas_call(
        kern,
        grid=(b, n_heads, s // t),
        in_specs=[pl.BlockSpec(lam_params.shape, lambda bi, h, qi: (0, 0)),
                  pl.BlockSpec((1, dv), lambda bi, h, qi: (0, 0)),
                  pl.BlockSpec((1, t, dv), lambda bi, h, qi: (bi, qi, h)),
                  pl.BlockSpec((1, s, dv), lambda bi, h, qi: (bi, 0, n_heads + h)),
                  pl.BlockSpec((1, s, dv), lambda bi, h, qi: (bi, 0, 2 * n_heads + h)),
                  pl.BlockSpec((1, 2, 2 * t, t), lambda bi, h, qi: (h, 0, 0, 0))],
        out_specs=pl.BlockSpec((1, t, dv), lambda bi, h, qi: (bi, qi, h)),
        out_shape=jax.ShapeDtypeStruct((b, s, n_heads * dv), BF16),
        scratch_shapes=[pltpu.VMEM((2 * t, 1), F32), pltpu.VMEM((2 * t, 1), F32),
                        pltpu.VMEM((2 * t, dv), F32)],
        compiler_params=_cparams(("parallel", "parallel", "arbitrary")),
        name="diff_attention",
    )(lam_params, subln_g, proj3, proj3, proj3, bias)


CONV_HALO = 32
CONV_ROW_CHUNK = 64


def _conv_kernel(a_ref, g_ref, ah_ref, gh_ref, bglu_ref, w_ref, cb_ref, lng_ref, lnb_ref, o_ref,
                 h_ref, c_ref, *, ts, kw, ch):
    i = pl.program_id(1)
    ba = bglu_ref[:, :ch]
    bg = bglu_ref[:, ch:]

    def glu(a, g):
        return (a.astype(F32) + ba) * jax.nn.sigmoid(g.astype(F32) + bg)

    halo = glu(ah_ref[0], gh_ref[0])
    h_ref[0:CONV_HALO, :] = jnp.where(i > 0, halo, jnp.zeros_like(halo))
    h_ref[CONV_HALO:, :] = glu(a_ref[0], g_ref[0])

    rc = min(CONV_ROW_CHUNK, ts)
    for c in range(ch // LANES):
        cs = slice(c * LANES, (c + 1) * LANES)
        wc = w_ref[:, cs]
        cb = cb_ref[:, cs]

        for r0 in range(0, ts, rc):
            acc = jnp.zeros((rc, LANES), F32)
            for k in range(kw):
                off = r0 + CONV_HALO - (kw - 1) + k
                acc = acc + h_ref[off:off + rc, cs] * wc[k:k + 1, :]
            c_ref[r0:r0 + rc, cs] = acc + cb

    y = _ln_rows(c_ref[...], lng_ref[...], lnb_ref[...])
    o_ref[0] = (y * jax.nn.sigmoid(y)).astype(o_ref.dtype)


def _conformer_conv(proj3, b_glu, conv_w, conv_b, ln_g, ln_b, glu_col0):
    b, s, _ = proj3.shape
    kw, ch = conv_w.shape
    assert kw - 1 <= CONV_HALO and glu_col0 % ch == 0
    ts = _pick(256, s)
    cb0 = glu_col0 // ch
    hb = ts // CONV_HALO
    kern = functools.partial(_conv_kernel, ts=ts, kw=kw, ch=ch)
    halo_idx = lambda off: (lambda bi, i: (bi, jnp.maximum(i * hb - 1, 0), off))
    vec = lambda n: pl.BlockSpec((1, n), lambda bi, i: (0, 0))
    return pl.pallas_call(
        kern,
        grid=(b, s // ts),
        in_specs=[pl.BlockSpec((1, ts, ch), lambda bi, i: (bi, i, cb0)),
                  pl.BlockSpec((1, ts, ch), lambda bi, i: (bi, i, cb0 + 1)),
                  pl.BlockSpec((1, CONV_HALO, ch), halo_idx(cb0)),
                  pl.BlockSpec((1, CONV_HALO, ch), halo_idx(cb0 + 1)),
                  vec(2 * ch),
                  pl.BlockSpec((kw, ch), lambda bi, i: (0, 0)),
                  vec(ch), vec(ch), vec(ch)],
        out_specs=pl.BlockSpec((1, ts, ch), lambda bi, i: (bi, i, 0)),
        out_shape=jax.ShapeDtypeStruct((b, s, ch), BF16),
        scratch_shapes=[pltpu.VMEM((CONV_HALO + ts, ch), F32), pltpu.VMEM((ts, ch), F32)],
        compiler_params=_cparams(("parallel", "arbitrary")),
        name="conformer_conv",
    )(proj3, proj3, proj3, proj3, b_glu, conv_w, conv_b, ln_g, ln_b)


def _outproj_kernel(a_ref, c_ref, wa_ref, wc_ref, x_ref, g_ref, b_ref, o_ref, ob_ref):
    mix = (jnp.dot(a_ref[...], wa_ref[...], preferred_element_type=F32)
           + jnp.dot(c_ref[...], wc_ref[...], preferred_element_type=F32))
    y = _ln_rows(DEEPNORM_ALPHA * x_ref[...] + mix, g_ref[...], b_ref[...])
    o_ref[...] = y
    ob_ref[...] = y.astype(BF16)


def _out_proj_ln(attn2, conv2, w_out, x2, ln_g, ln_b):
    m, d = x2.shape
    da = attn2.shape[1]
    dc = conv2.shape[1]
    tm = _pick(512, m)
    vec = pl.BlockSpec((1, d), lambda i: (0, 0))
    return pl.pallas_call(
        _outproj_kernel,
        grid=(m // tm,),
        in_specs=[pl.BlockSpec((tm, da), lambda i: (i, 0)),
                  pl.BlockSpec((tm, dc), lambda i: (i, 0)),
                  pl.BlockSpec((da, d), lambda i: (0, 0)),
                  pl.BlockSpec((dc, d), lambda i: (0, 0)),
                  pl.BlockSpec((tm, d), lambda i: (i, 0)),
                  vec, vec],
        out_specs=[pl.BlockSpec((tm, d), lambda i: (i, 0)),
                   pl.BlockSpec((tm, d), lambda i: (i, 0))],
        out_shape=[jax.ShapeDtypeStruct((m, d), F32), jax.ShapeDtypeStruct((m, d), BF16)],
        compiler_params=_cparams(("parallel",)),
        name="out_proj_ln1",
    )(attn2, conv2, w_out[:da], w_out[da:], x2, ln_g, ln_b)


def _top_desc(s, k):
    rows = lax.broadcasted_iota(jnp.int32, (k, s.shape[1]), 0)
    tops = jnp.zeros((k, s.shape[1]), F32)
    rank = jnp.full(s.shape, float(k), F32)
    work = s
    for a in range(k):
        m = jnp.max(work, axis=0, keepdims=True)
        hit = work == m
        rank = jnp.where(hit, float(a), rank)
        work = jnp.where(hit, -jnp.inf, work)
        tops = jnp.where(rows == a, m, tops)
    return tops, rank


def _kth_largest(c, k):
    work = c
    m = None
    for _ in range(k):
        m = jnp.max(work, axis=0, keepdims=True)
        work = jnp.where(work == m, -jnp.inf, work)
    return m


def _peer_select_kernel(x_ref, wq_ref, keys_ref, n_ref, ea_ref, rb_ref, eb_ref, s_ref,
                        *, n_heads, nk, tt):
    k = PEER_TOPK
    q = jnp.dot(x_ref[...], wq_ref[...], preferred_element_type=F32).astype(BF16)
    hd = keys_ref.shape[2]
    for hc in range(2 * n_heads):
        s_ref[hc] = lax.dot_general(keys_ref[hc], q[:, hc * hd:(hc + 1) * hd],
                                    (((1,), (1,)), ((), ())), preferred_element_type=F32)

    def body(it, carry):
        h = it // (tt // LANES)
        c0 = pl.multiple_of((it % (tt // LANES)) * LANES, LANES)
        cols = pl.ds(c0, LANES)
        sa = s_ref[2 * h, :, cols]
        sb = s_ref[2 * h + 1, :, cols]
        s1, ra = _top_desc(sa, k)
        s2, rb = _top_desc(sb, k)
        half = k // 2
        groups = [s1[0:1] + s2]
        groups += [s1[a:a + 1] + s2[:half] for a in range(1, half)]
        groups += [s1[half:] + s2[0:1]]
        cand = jnp.concatenate(groups, axis=0)
        tau = _kth_largest(cand, k)
        sel = cand >= tau
        top = s1[0:1] + s2[0:1]
        z = jnp.sum(jnp.where(sel, jnp.exp(cand - top), 0.0), axis=0, keepdims=True)
        self32 = jnp.where(sel, 1.0, 0.0)
        counts = [jnp.sum(self32[0:k], axis=0, keepdims=True)]
        counts += [jnp.sum(self32[k + (a - 1) * half:k + a * half], axis=0, keepdims=True)
                   for a in range(1, half)]
        tail = self32[k + (half - 1) * half:]
        n_i = jnp.zeros(sa.shape, F32)
        for a in range(k):
            n_a = counts[a] if a < half else tail[a - half:a - half + 1]
            n_i = jnp.where(ra == float(a), n_a, n_i)
        n_ref[h, :, cols] = n_i
        ea_ref[h, :, cols] = jnp.exp(sa - s1[0:1]) / z
        rb_ref[h, :, cols] = rb.astype(BF16)
        eb_ref[h, :, cols] = jnp.exp(sb - s2[0:1]).astype(BF16)
        return carry

    lax.fori_loop(0, n_heads * (tt // LANES), body, 0)


def _peer_select(x1b, wq, keys2):
    m, d = x1b.shape
    hc, nk, hd = keys2.shape
    n_heads = hc // 2
    tt = _pick(512, m)
    kern = functools.partial(_peer_select_kernel, n_heads=n_heads, nk=nk, tt=tt)
    out_spec = pl.BlockSpec((n_heads, nk, tt), lambda i: (0, 0, i))
    shp = lambda dt: jax.ShapeDtypeStruct((n_heads, nk, m), dt)
    return pl.pallas_call(
        kern,
        grid=(m // tt,),
        in_specs=[pl.BlockSpec((tt, d), lambda i: (i, 0)),
                  pl.BlockSpec(wq.shape, lambda i: (0, 0)),
                  pl.BlockSpec(keys2.shape, lambda i: (0, 0, 0))],
        out_specs=[out_spec, out_spec, out_spec, out_spec],
        out_shape=[shp(F32), shp(F32), shp(BF16), shp(BF16)],
        scratch_shapes=[pltpu.VMEM((hc, nk, tt), F32)],
        compiler_params=_cparams(("parallel",)),
        name="peer_select",
    )(x1b, wq, keys2)


def _peer_dense_kernel(xb_ref, u_ref, vt_ref, n_ref, ea_ref, rb_ref, eb_ref, x_ref, g_ref, b_ref,
                       o_ref, acc_ref, z_ref, w_ref, *, n_heads, ib, nk):
    c = pl.program_id(1)

    @pl.when(c == 0)
    def _():
        acc_ref[...] = jnp.zeros(acc_ref.shape, F32)

    z_ref[...] = lax.dot_general(u_ref[...], xb_ref[...], (((1,), (1,)), ((), ())),
                                 preferred_element_type=F32)

    def ibody(i, carry):
        r0 = pl.multiple_of(i * nk, nk)
        z = z_ref[pl.ds(r0, nk), :]
        act = 0.5 * z * (1.0 + lax.erf(z * (2.0 ** -0.5)))
        ig = c * ib + i
        w = jnp.zeros(z.shape, BF16)
        for h in range(n_heads):
            n_row = n_ref[h, pl.ds(ig, 1), :].astype(BF16)
            ea_row = ea_ref[h, pl.ds(ig, 1), :].astype(BF16)
            eb = eb_ref[h]
            w = w + jnp.where(rb_ref[h] < n_row, eb, jnp.zeros_like(eb)) * ea_row
        w_ref[pl.ds(r0, nk), :] = act.astype(BF16) * w
        return carry

    lax.fori_loop(0, ib, ibody, 0)
    acc_ref[...] += jnp.dot(vt_ref[...], w_ref[...], preferred_element_type=F32)

    @pl.when(c == pl.num_programs(1) - 1)
    def _():
        y = DEEPNORM_ALPHA * x_ref[...] + acc_ref[...].T
        o_ref[...] = _ln_rows(y, g_ref[...], b_ref[...])


def _peer_dense(x1b, x1, u, vt, sel, ln_g, ln_b, nk):
    m, d = x1.shape
    ne = u.shape[0]
    n_i, ea, rb, eb = sel
    n_heads = n_i.shape[0]
    tt = _pick(512, m)
    ib = 4
    ec = ib * nk
    kern = functools.partial(_peer_dense_kernel, n_heads=n_heads, ib=ib, nk=nk)
    sel_spec = pl.BlockSpec((n_heads, nk, tt), lambda t, c: (0, 0, t))
    vec = pl.BlockSpec((1, d), lambda t, c: (0, 0))
    return pl.pallas_call(
        kern,
        grid=(m // tt, ne // ec),
        in_specs=[pl.BlockSpec((tt, d), lambda t, c: (t, 0)),
                  pl.BlockSpec((ec, d), lambda t, c: (c, 0)),
                  pl.BlockSpec((d, ec), lambda t, c: (0, c)),
                  sel_spec, sel_spec, sel_spec, sel_spec,
                  pl.BlockSpec((tt, d), lambda t, c: (t, 0)),
                  vec, vec],
        out_specs=pl.BlockSpec((tt, d), lambda t, c: (t, 0)),
        out_shape=jax.ShapeDtypeStruct((m, d), F32),
        scratch_shapes=[pltpu.VMEM((d, tt), F32), pltpu.VMEM((ec, tt), F32),
                        pltpu.VMEM((ec, tt), BF16)],
        compiler_params=_cparams(("parallel", "arbitrary")),
        name="peer_experts_ln2",
    )(x1b, u, vt, n_i, ea, rb, eb, x1, ln_g, ln_b)


def kernel(x, w_in, b_glu, conv_w, conv_b, conv_ln_g, conv_ln_b, lambda_q1, lambda_k1, lambda_q2,
           lambda_k2, subln_g, rel_bias, w_out, ln1_g, ln1_b, peer_wq, peer_keys, peer_u, peer_v,
           ln2_g, ln2_b):
    b, s, d = x.shape
    n_layers = w_in.shape[0]
    n_heads = rel_bias.shape[1]
    qk_cols = n_heads * 2 * ATTN_HEAD_DIM
    glu_col0 = 2 * qk_cols + n_heads * ATTN_VALUE_DIM
    t_attn = _pick(512, s)
    bias = _bias_tiles(rel_bias, t_attn)
    row = lambda v: v.reshape(1, -1).astype(F32)

    x2 = x.reshape(b * s, d)
    for layer in range(n_layers):
        lambda_init = 0.8 - 0.6 * math.exp(-0.3 * layer)
        proj3 = _in_proj(x2, w_in[layer].astype(BF16)).reshape(b, s, -1)
        lam_params = jnp.stack([lambda_q1[layer], lambda_k1[layer],
                                lambda_q2[layer], lambda_k2[layer]]).astype(F32)
        attn = _attention(proj3, lam_params, row(subln_g[layer]), bias, n_heads, lambda_init)
        conv = _conformer_conv(proj3, row(b_glu[layer]), conv_w[layer].astype(F32),
                               row(conv_b[layer]), row(conv_ln_g[layer]), row(conv_ln_b[layer]),
                               glu_col0)
        x1, x1b = _out_proj_ln(attn.reshape(b * s, -1), conv.reshape(b * s, -1),
                               w_out[layer].astype(BF16), x2, row(ln1_g[layer]), row(ln1_b[layer]))
        keys = peer_keys[layer]
        ph, _, nk, hd = keys.shape
        sel = _peer_select(x1b, peer_wq[layer].astype(BF16),
                           keys.reshape(2 * ph, nk, hd).astype(BF16))
        x2 = _peer_dense(x1b, x1, peer_u[layer].astype(BF16), peer_v[layer].astype(BF16).T, sel,
                         row(ln2_g[layer]), row(ln2_b[layer]), nk)
    return x2.reshape(b, s, d)
```

```python
import functools
import math

import numpy as np

import jax
import jax.numpy as jnp
from jax import lax
from jax.experimental import pallas as pl
from jax.experimental.pallas import tpu as pltpu

F32 = jnp.float32
BF16 = jnp.bfloat16

ATTN_HEAD_DIM = 64
ATTN_VALUE_DIM = 2 * ATTN_HEAD_DIM
REL_MAX_DISTANCE = 128
PEER_TOPK = 16
LN_EPS = 1e-5
DEPTH = 1
DEEPNORM_ALPHA = (2.0 * DEPTH) ** 0.25

LANES = 128
VMEM_LIMIT_BYTES = 56 * 1024 * 1024


def _pick(limit, total, quantum=128):
    if total <= limit:
        return total
    t = (limit // quantum) * quantum
    while t > quantum and total % t:
        t -= quantum
    assert total % t == 0, (limit, total)
    return t


def _cparams(sem):
    return pltpu.CompilerParams(dimension_semantics=sem, vmem_limit_bytes=VMEM_LIMIT_BYTES)


def _ln_rows(y, g, b):
    mu = jnp.mean(y, axis=-1, keepdims=True)
    d = y - mu
    var = jnp.mean(d * d, axis=-1, keepdims=True)
    return d * lax.rsqrt(var + LN_EPS) * g + b


def _inproj_kernel(x_ref, w_ref, o_ref, xb_ref, *, transposed):
    @pl.when(pl.program_id(1) == 0)
    def _():
        xb_ref[...] = x_ref[...].astype(BF16)

    if transposed:
        o = lax.dot_general(w_ref[...], xb_ref[...], (((1,), (1,)), ((), ())),
                            preferred_element_type=F32)
    else:
        o = jnp.dot(xb_ref[...], w_ref[...], preferred_element_type=F32)
    o_ref[...] = o.astype(o_ref.dtype)


def _in_proj(x2, w, transposed):
    m, d = x2.shape
    n = w.shape[0] if transposed else w.shape[1]
    tm = _pick(1024, m)
    tn = _pick(512, n)
    if transposed:
        w_spec = pl.BlockSpec((tn, d), lambda i, j: (j, 0))
        o_spec = pl.BlockSpec((tn, tm), lambda i, j: (j, i))
        o_shape = jax.ShapeDtypeStruct((n, m), BF16)
    else:
        w_spec = pl.BlockSpec((d, tn), lambda i, j: (0, j))
        o_spec = pl.BlockSpec((tm, tn), lambda i, j: (i, j))
        o_shape = jax.ShapeDtypeStruct((m, n), BF16)
    return pl.pallas_call(
        functools.partial(_inproj_kernel, transposed=transposed),
        grid=(m // tm, n // tn),
        in_specs=[pl.BlockSpec((tm, d), lambda i, j: (i, 0)), w_spec],
        out_specs=o_spec,
        out_shape=o_shape,
        scratch_shapes=[pltpu.VMEM((tm, d), BF16)],
        compiler_params=_cparams(("parallel", "arbitrary")),
        name="in_proj_t" if transposed else "in_proj",
    )(x2, w)


LOG2E = math.log2(math.e)
ONES_ROWS = 16


def _t5_bucket_np(dist, num_buckets):
    max_exact = num_buckets // 2
    n_f = np.maximum(dist, 1).astype(np.float32)
    large = max_exact + (np.log(n_f / np.float32(max_exact))
                         / np.float32(math.log(REL_MAX_DISTANCE / max_exact))
                         * np.float32(num_buckets - max_exact)).astype(np.int32)
    large = np.minimum(large, num_buckets - 1)
    return np.where(dist < max_exact, dist, large).astype(np.int32)


def _bias_kernel(rb_ref, bucket_ref, o_ref, *, nb):
    h = pl.program_id(0)
    bucket = bucket_ref[...]
    tile = jnp.zeros(bucket.shape, F32)
    for bkt in range(nb):
        tile = jnp.where(bucket == bkt, rb_ref[bkt, h], tile)
    tile = (tile - rb_ref[nb - 1, h]) * LOG2E
    o_ref[0] = jnp.where(bucket < 0, -jnp.inf, tile)


def _bias_tiles(rel_bias, t):
    nb, n_heads = rel_bias.shape
    assert t + 1 >= REL_MAX_DISTANCE
    key = np.arange(t, dtype=np.int32)[:, None]
    qry = np.arange(t, dtype=np.int32)[None, :]
    d_diag = qry - key
    diag = np.where(d_diag >= 0, _t5_bucket_np(np.maximum(d_diag, 0), nb), -1)
    prev = _t5_bucket_np(d_diag + t, nb)
    buckets = jnp.asarray(np.stack([prev, diag]).astype(np.int32))
    return pl.pallas_call(
        functools.partial(_bias_kernel, nb=nb),
        grid=(n_heads,),
        in_specs=[pl.BlockSpec(memory_space=pltpu.SMEM),
                  pl.BlockSpec((2, t, t), lambda h: (0, 0, 0))],
        out_specs=pl.BlockSpec((1, 2, t, t), lambda h: (h, 0, 0, 0)),
        out_shape=jax.ShapeDtypeStruct((n_heads, 2, t, t), F32),
        compiler_params=_cparams(("arbitrary",)),
        name="rel_bias_tiles",
    )(rel_bias.astype(F32), buckets)


def _attn_kernel(lam_ref, g_ref, qt_ref, k_ref, vt_ref, bias_ref, o_ref, vext_ref,
                 *, t, lambda_init):
    qi = pl.program_id(2)
    dv = ATTN_VALUE_DIM

    @pl.when(qi == 0)
    def _():
        vext_ref[0:dv, :] = vt_ref[...]
        vext_ref[dv:, :] = jnp.ones((ONES_ROWS, vext_ref.shape[1]), BF16)

    qf = qt_ref[...].astype(F32) * (ATTN_HEAD_DIM ** -0.5 * LOG2E)
    row = lax.broadcasted_iota(jnp.int32, qf.shape, 0)
    comps = (jnp.where(row < ATTN_HEAD_DIM, qf, 0.0).astype(BF16),
             jnp.where(row >= ATTN_HEAD_DIM, qf, 0.0).astype(BF16))

    def qk(j):
        kj = k_ref[0, pl.ds(pl.multiple_of(j * t, t), t), :]
        return tuple(jnp.dot(kj, comps[c], preferred_element_type=F32) for c in range(2))

    def step(j, bias, state, prefetch):
        scores, chains = state
        nxt = qk(j + 1) if prefetch else scores
        vj = vext_ref[:, pl.ds(pl.multiple_of(j * t, t), t)]
        stats = []
        for c in range(2):
            s = scores[c] if bias is None else scores[c] + bias
            m_prev = chains[c][0]
            m_new = jnp.maximum(m_prev, jnp.max(s, axis=0, keepdims=True))
            stats.append((m_new, jnp.exp2(m_prev - m_new), jnp.exp2(s - m_new).astype(BF16)))
        return nxt, tuple((m_new, alpha * chains[c][1] + jnp.dot(vj, p, preferred_element_type=F32))
                          for c, (m_new, alpha, p) in enumerate(stats))

    init = (jnp.full((1, t), -jnp.inf, F32), jnp.zeros((dv + ONES_ROWS, t), F32))
    state = (qk(0), (init, init))
    state = lax.fori_loop(0, jnp.maximum(qi - 1, 0), lambda j, st: step(j, None, st, True), state)
    state = lax.cond(qi >= 1, lambda st: step(qi - 1, bias_ref[0, 0], st, True), lambda st: st, state)
    _, chains = step(qi, bias_ref[0, 1], state, False)

    lam_p = lam_ref[...]
    lam = (jnp.exp(jnp.sum(lam_p[0:1] * lam_p[1:2], axis=1, keepdims=True))
           - jnp.exp(jnp.sum(lam_p[2:3] * lam_p[3:4], axis=1, keepdims=True)) + lambda_init)
    a1 = chains[0][1]
    a2 = chains[1][1]
    o = a1[:dv] / a1[dv:dv + 1] - lam * (a2[:dv] / a2[dv:dv + 1])
    y = o * lax.rsqrt(jnp.mean(o * o, axis=0, keepdims=True) + LN_EPS)
    y = y * (g_ref[...] * (1.0 - lambda_init))
    o_ref[0] = y.T.astype(o_ref.dtype)


def _attention(qvt, k3, lam_params, subln_g_col, bias, n_heads, lambda_init):
    b, s, _ = k3.shape
    t = bias.shape[-1]
    dv = ATTN_VALUE_DIM
    nq = s // t
    kern = functools.partial(_attn_kernel, t=t, lambda_init=lambda_init)
    return pl.pallas_call(
        kern,
        grid=(b, n_heads, nq),
        in_specs=[pl.BlockSpec(lam_params.shape, lambda bi, h, qi: (0, 0)),
                  pl.BlockSpec((dv, 1), lambda bi, h, qi: (0, 0)),
                  pl.BlockSpec((dv, t), lambda bi, h, qi: (h, bi * nq + qi)),
                  pl.BlockSpec((1, s, dv), lambda bi, h, qi: (bi, 0, h)),
                  pl.BlockSpec((dv, s), lambda bi, h, qi: (n_heads + h, bi)),
                  pl.BlockSpec((1, 2, t, t), lambda bi, h, qi: (h, 0, 0, 0))],
        out_specs=pl.BlockSpec((1, t, dv), lambda bi, h, qi: (bi, qi, h)),
        out_shape=jax.ShapeDtypeStruct((b, s, n_heads * dv), BF16),
        scratch_shapes=[pltpu.VMEM((dv + ONES_ROWS, s), BF16)],
        compiler_params=_cparams(("parallel", "parallel", "arbitrary")),
        name="diff_attention",
    )(lam_params, subln_g_col, qvt, k3, qvt, bias)


CONV_HALO = 32
CONV_ROW_CHUNK = 64


def _conv_kernel(a_ref, g_ref, ah_ref, gh_ref, bglu_ref, w_ref, cb_ref, lng_ref, lnb_ref, o_ref,
                 h_ref, c_ref, *, ts, kw, ch):
    i = pl.program_id(1)
    ba = bglu_ref[:, :ch]
    bg = bglu_ref[:, ch:]

    def glu(a, g):
        return (a.astype(F32) + ba) * jax.nn.sigmoid(g.astype(F32) + bg)

    halo = glu(ah_ref[0], gh_ref[0])
    h_ref[0:CONV_HALO, :] = jnp.where(i > 0, halo, jnp.zeros_like(halo))
    h_ref[CONV_HALO:, :] = glu(a_ref[0], g_ref[0])

    rc = min(CONV_ROW_CHUNK, ts)
    for c in range(ch // LANES):
        cs = slice(c * LANES, (c + 1) * LANES)
        wc = w_ref[:, cs]
        cb = cb_ref[:, cs]

        for r0 in range(0, ts, rc):
            acc = jnp.zeros((rc, LANES), F32)
            for k in range(kw):
                off = r0 + CONV_HALO - (kw - 1) + k
                acc = acc + h_ref[off:off + rc, cs] * wc[k:k + 1, :]
            c_ref[r0:r0 + rc, cs] = acc + cb

    y = _ln_rows(c_ref[...], lng_ref[...], lnb_ref[...])
    o_ref[0] = (y * jax.nn.sigmoid(y)).astype(o_ref.dtype)


def _conformer_conv(proj3, b_glu, conv_w, conv_b, ln_g, ln_b, glu_col0):
    b, s, _ = proj3.shape
    kw, ch = conv_w.shape
    assert kw - 1 <= CONV_HALO and glu_col0 % ch == 0
    ts = _pick(256, s)
    cb0 = glu_col0 // ch
    hb = ts // CONV_HALO
    kern = functools.partial(_conv_kernel, ts=ts, kw=kw, ch=ch)
    halo_idx = lambda off: (lambda bi, i: (bi, jnp.maximum(i * hb - 1, 0), off))
    vec = lambda n: pl.BlockSpec((1, n), lambda bi, i: (0, 0))
    return pl.pallas_call(
        kern,
        grid=(b, s // ts),
        in_specs=[pl.BlockSpec((1, ts, ch), lambda bi, i: (bi, i, cb0)),
                  pl.BlockSpec((1, ts, ch), lambda bi, i: (bi, i, cb0 + 1)),
                  pl.BlockSpec((1, CONV_HALO, ch), halo_idx(cb0)),
                  pl.BlockSpec((1, CONV_HALO, ch), halo_idx(cb0 + 1)),
                  vec(2 * ch),
                  pl.BlockSpec((kw, ch), lambda bi, i: (0, 0)),
                  vec(ch), vec(ch), vec(ch)],
        out_specs=pl.BlockSpec((1, ts, ch), lambda bi, i: (bi, i, 0)),
        out_shape=jax.ShapeDtypeStruct((b, s, ch), BF16),
        scratch_shapes=[pltpu.VMEM((CONV_HALO + ts, ch), F32), pltpu.VMEM((ts, ch), F32)],
        compiler_params=_cparams(("parallel", "arbitrary")),
        name="conformer_conv",
    )(proj3, proj3, proj3, proj3, b_glu, conv_w, conv_b, ln_g, ln_b)


def _outproj_kernel(a_ref, c_ref, wa_ref, wc_ref, x_ref, g_ref, b_ref, o_ref, ob_ref):
    mix = (jnp.dot(a_ref[...], wa_ref[...], preferred_element_type=F32)
           + jnp.dot(c_ref[...], wc_ref[...], preferred_element_type=F32))
    y = _ln_rows(DEEPNORM_ALPHA * x_ref[...] + mix, g_ref[...], b_ref[...])
    o_ref[...] = y
    ob_ref[...] = y.astype(BF16)


def _out_proj_ln(attn2, conv2, w_out, x2, ln_g, ln_b):
    m, d = x2.shape
    da = attn2.shape[1]
    dc = conv2.shape[1]
    tm = _pick(512, m)
    vec = pl.BlockSpec((1, d), lambda i: (0, 0))
    return pl.pallas_call(
        _outproj_kernel,
        grid=(m // tm,),
        in_specs=[pl.BlockSpec((tm, da), lambda i: (i, 0)),
                  pl.BlockSpec((tm, dc), lambda i: (i, 0)),
                  pl.BlockSpec((da, d), lambda i: (0, 0)),
                  pl.BlockSpec((dc, d), lambda i: (0, 0)),
                  pl.BlockSpec((tm, d), lambda i: (i, 0)),
                  vec, vec],
        out_specs=[pl.BlockSpec((tm, d), lambda i: (i, 0)),
                   pl.BlockSpec((tm, d), lambda i: (i, 0))],
        out_shape=[jax.ShapeDtypeStruct((m, d), F32), jax.ShapeDtypeStruct((m, d), BF16)],
        compiler_params=_cparams(("parallel",)),
        name="out_proj_ln1",
    )(attn2, conv2, w_out[:da], w_out[da:], x2, ln_g, ln_b)


def _top_desc(s, k):
    rows = lax.broadcasted_iota(jnp.int32, (k, s.shape[1]), 0)
    tops = jnp.zeros((k, s.shape[1]), F32)
    rank = jnp.full(s.shape, float(k), F32)
    work = s
    for a in range(k):
        m = jnp.max(work, axis=0, keepdims=True)
        hit = work == m
        rank = jnp.where(hit, float(a), rank)
        work = jnp.where(hit, -jnp.inf, work)
        tops = jnp.where(rows == a, m, tops)
    return tops, rank


def _kth_largest(c, k):
    work = c
    m = None
    for _ in range(k):
        m = jnp.max(work, axis=0, keepdims=True)
        work = jnp.where(work == m, -jnp.inf, work)
    return m


def _peer_select_kernel(x_ref, wq_ref, keys_ref, n_ref, ea_ref, rb_ref, eb_ref, s_ref,
                        *, n_heads, nk, tt):
    k = PEER_TOPK
    q = jnp.dot(x_ref[...], wq_ref[...], preferred_element_type=F32).astype(BF16)
    hd = keys_ref.shape[2]
    for hc in range(2 * n_heads):
        s_ref[hc] = lax.dot_general(keys_ref[hc], q[:, hc * hd:(hc + 1) * hd],
                                    (((1,), (1,)), ((), ())), preferred_element_type=F32)

    def body(it, carry):
        h = it // (tt // LANES)
        c0 = pl.multiple_of((it % (tt // LANES)) * LANES, LANES)
        cols = pl.ds(c0, LANES)
        sa = s_ref[2 * h, :, cols]
        sb = s_ref[2 * h + 1, :, cols]
        s1, ra = _top_desc(sa, k)
        s2, rb = _top_desc(sb, k)
        half = k // 2
        groups = [s1[0:1] + s2]
        groups += [s1[a:a + 1] + s2[:half] for a in range(1, half)]
        groups += [s1[half:] + s2[0:1]]
        cand = jnp.concatenate(groups, axis=0)
        tau = _kth_largest(cand, k)
        sel = cand >= tau
        top = s1[0:1] + s2[0:1]
        z = jnp.sum(jnp.where(sel, jnp.exp(cand - top), 0.0), axis=0, keepdims=True)
        self32 = jnp.where(sel, 1.0, 0.0)
        counts = [jnp.sum(self32[0:k], axis=0, keepdims=True)]
        counts += [jnp.sum(self32[k + (a - 1) * half:k + a * half], axis=0, keepdims=True)
                   for a in range(1, half)]
        tail = self32[k + (half - 1) * half:]
        n_i = jnp.zeros(sa.shape, F32)
        for a in range(k):
            n_a = counts[a] if a < half else tail[a - half:a - half + 1]
            n_i = jnp.where(ra == float(a), n_a, n_i)
        n_ref[h, :, cols] = n_i
        ea_ref[h, :, cols] = jnp.exp(sa - s1[0:1]) / z
        rb_ref[h, :, cols] = rb.astype(BF16)
        eb_ref[h, :, cols] = jnp.exp(sb - s2[0:1]).astype(BF16)
        return carry

    lax.fori_loop(0, n_heads * (tt // LANES), body, 0)


def _peer_select(x1b, wq, keys2):
    m, d = x1b.shape
    hc, nk, hd = keys2.shape
    n_heads = hc // 2
    tt = _pick(512, m)
    kern = functools.partial(_peer_select_kernel, n_heads=n_heads, nk=nk, tt=tt)
    out_spec = pl.BlockSpec((n_heads, nk, tt), lambda i: (0, 0, i))
    shp = lambda dt: jax.ShapeDtypeStruct((n_heads, nk, m), dt)
    return pl.pallas_call(
        kern,
        grid=(m // tt,),
        in_specs=[pl.BlockSpec((tt, d), lambda i: (i, 0)),
                  pl.BlockSpec(wq.shape, lambda i: (0, 0)),
                  pl.BlockSpec(keys2.shape, lambda i: (0, 0, 0))],
        out_specs=[out_spec, out_spec, out_spec, out_spec],
        out_shape=[shp(F32), shp(F32), shp(BF16), shp(BF16)],
        scratch_shapes=[pltpu.VMEM((hc, nk, tt), F32)],
        compiler_params=_cparams(("parallel",)),
        name="peer_select",
    )(x1b, wq, keys2)


PEER_STEP_BLOCKS = 8
PEER_SLAB_BLOCKS = 2


def _peer_dense_kernel(xb_ref, u_ref, vt_ref, n_ref, ea_ref, rb_ref, eb_ref, x_ref, g_ref, b_ref,
                       o_ref, acc_ref, z_ref, w_ref, *, n_heads, ib, nk):
    c = pl.program_id(1)

    @pl.when(c == 0)
    def _():
        acc_ref[...] = jnp.zeros(acc_ref.shape, F32)

    xb = xb_ref[...]
    slab = PEER_SLAB_BLOCKS * nk
    for r0 in range(0, ib * nk, slab):
        z_ref[r0:r0 + slab, :] = lax.dot_general(u_ref[r0:r0 + slab, :], xb, (((1,), (1,)), ((), ())),
                                                 preferred_element_type=F32)

    g0 = pl.multiple_of(c * ib, ib)
    n_rows = [n_ref[h, pl.ds(g0, ib), :] for h in range(n_heads)]
    ea_rows = [ea_ref[h, pl.ds(g0, ib), :] for h in range(n_heads)]
    half = (ib // 2) * nk
    pv = None
    for i in range(ib):
        r0 = i * nk
        z = z_ref[r0:r0 + nk, :]
        act = 0.5 * z * (1.0 + lax.erf(z * (2.0 ** -0.5)))
        w = jnp.zeros(z.shape, BF16)
        for h in range(n_heads):
            eb = eb_ref[h]
            hit = rb_ref[h] < n_rows[h][i:i + 1, :].astype(BF16)
            w = w + jnp.where(hit, eb, jnp.zeros_like(eb)) * ea_rows[h][i:i + 1, :].astype(BF16)
        w_ref[r0:r0 + nk, :] = act.astype(BF16) * w
        if (i + 1) * nk % half == 0:
            h0 = (i + 1) * nk - half
            part = jnp.dot(vt_ref[:, h0:h0 + half], w_ref[h0:h0 + half, :], preferred_element_type=F32)
            pv = part if pv is None else pv + part
    acc_ref[...] += pv

    @pl.when(c == pl.num_programs(1) - 1)
    def _():
        y = DEEPNORM_ALPHA * x_ref[...] + acc_ref[...].T
        o_ref[...] = _ln_rows(y, g_ref[...], b_ref[...])


def _peer_dense(x1b, x1, u, vt, sel, ln_g, ln_b, nk):
    m, d = x1.shape
    ne = u.shape[0]
    n_i, ea, rb, eb = sel
    n_heads = n_i.shape[0]
    tt = _pick(512, m)
    ib = PEER_STEP_BLOCKS
    ec = ib * nk
    kern = functools.partial(_peer_dense_kernel, n_heads=n_heads, ib=ib, nk=nk)
    once = pl.Buffered(1)
    sel_spec = pl.BlockSpec((n_heads, nk, tt), lambda t, c: (0, 0, t), pipeline_mode=once)
    vec = pl.BlockSpec((1, d), lambda t, c: (0, 0))
    return pl.pallas_call(
        kern,
        grid=(m // tt, ne // ec),
        in_specs=[pl.BlockSpec((tt, d), lambda t, c: (t, 0)),
                  pl.BlockSpec((ec, d), lambda t, c: (c, 0)),
                  pl.BlockSpec((d, ec), lambda t, c: (0, c)),
                  sel_spec, sel_spec, sel_spec, sel_spec,
                  pl.BlockSpec((tt, d), lambda t, c: (t, 0), pipeline_mode=once),
                  vec, vec],
        out_specs=pl.BlockSpec((tt, d), lambda t, c: (t, 0)),
        out_shape=jax.ShapeDtypeStruct((m, d), F32),
        scratch_shapes=[pltpu.VMEM((d, tt), F32), pltpu.VMEM((ec, tt), F32),
                        pltpu.VMEM((ec, tt), BF16)],
        compiler_params=_cparams(("parallel", "arbitrary")),
        name="peer_experts_ln2",
    )(x1b, u, vt, n_i, ea, rb, eb, x1, ln_g, ln_b)


def kernel(x, w_in, b_glu, conv_w, conv_b, conv_ln_g, conv_ln_b, lambda_q1, lambda_k1, lambda_q2,
           lambda_k2, subln_g, rel_bias, w_out, ln1_g, ln1_b, peer_wq, peer_keys, peer_u, peer_v,
           ln2_g, ln2_b):
    b, s, d = x.shape
    n_layers = w_in.shape[0]
    n_heads = rel_bias.shape[1]
    qk_cols = n_heads * 2 * ATTN_HEAD_DIM
    v_cols = n_heads * ATTN_VALUE_DIM
    t_attn = _pick(512, s)
    bias = _bias_tiles(rel_bias, t_attn)
    row = lambda v: v.reshape(1, -1).astype(F32)

    x2 = x.reshape(b * s, d)
    for layer in range(n_layers):
        lambda_init = 0.8 - 0.6 * math.exp(-0.3 * layer)
        w = w_in[layer].astype(BF16)
        w_q, w_k, w_v, w_glu = (w[:, :qk_cols], w[:, qk_cols:2 * qk_cols],
                                w[:, 2 * qk_cols:2 * qk_cols + v_cols], w[:, 2 * qk_cols + v_cols:])
        kg3 = _in_proj(x2, jnp.concatenate([w_k, w_glu], axis=1), False).reshape(b, s, -1)
        qvt = _in_proj(x2, jnp.concatenate([w_q, w_v], axis=1).T, True)
        lam_params = jnp.stack([lambda_q1[layer], lambda_k1[layer],
                                lambda_q2[layer], lambda_k2[layer]]).astype(F32)
        attn = _attention(qvt, kg3, lam_params, subln_g[layer].reshape(-1, 1).astype(F32), bias,
                          n_heads, lambda_init)
        conv = _conformer_conv(kg3, row(b_glu[layer]), conv_w[layer].astype(F32),
                               row(conv_b[layer]), row(conv_ln_g[layer]), row(conv_ln_b[layer]),
                               qk_cols)
        x1, x1b = _out_proj_ln(attn.reshape(b * s, -1), conv.reshape(b * s, -1),
                               w_out[layer].astype(BF16), x2, row(ln1_g[layer]), row(ln1_b[layer]))
        keys = peer_keys[layer]
        ph, _, nk, hd = keys.shape
        sel = _peer_select(x1b, peer_wq[layer].astype(BF16),
                           keys.reshape(2 * ph, nk, hd).astype(BF16))
        x2 = _peer_dense(x1b, x1, peer_u[layer].astype(BF16), peer_v[layer].astype(BF16).T, sel,
                         row(ln2_g[layer]), row(ln2_b[layer]), nk)
    return x2.reshape(b, s, d)
```

```python
import functools
import math

import numpy as np

import jax
import jax.numpy as jnp
from jax import lax
from jax.experimental import pallas as pl
from jax.experimental.pallas import tpu as pltpu

F32 = jnp.float32
BF16 = jnp.bfloat16

ATTN_HEAD_DIM = 64
ATTN_VALUE_DIM = 2 * ATTN_HEAD_DIM
REL_MAX_DISTANCE = 128
PEER_TOPK = 16
LN_EPS = 1e-5
DEPTH = 1
DEEPNORM_ALPHA = (2.0 * DEPTH) ** 0.25

LANES = 128
VMEM_LIMIT_BYTES = 56 * 1024 * 1024


def _pick(limit, total, quantum=128):
    if total <= limit:
        return total
    t = (limit // quantum) * quantum
    while t > quantum and total % t:
        t -= quantum
    assert total % t == 0, (limit, total)
    return t


def _cparams(sem):
    return pltpu.CompilerParams(dimension_semantics=sem, vmem_limit_bytes=VMEM_LIMIT_BYTES)


def _ln_rows(y, g, b):
    mu = jnp.mean(y, axis=-1, keepdims=True)
    d = y - mu
    var = jnp.mean(d * d, axis=-1, keepdims=True)
    return d * lax.rsqrt(var + LN_EPS) * g + b


def _inproj_kernel(x_ref, w_ref, o_ref, xb_ref, *, transposed):
    @pl.when(pl.program_id(1) == 0)
    def _():
        xb_ref[...] = x_ref[...].astype(BF16)

    if transposed:
        o = lax.dot_general(w_ref[...], xb_ref[...], (((1,), (1,)), ((), ())),
                            preferred_element_type=F32)
    else:
        o = jnp.dot(xb_ref[...], w_ref[...], preferred_element_type=F32)
    o_ref[...] = o.astype(o_ref.dtype)


def _in_proj(x2, w, transposed):
    m, d = x2.shape
    n = w.shape[0] if transposed else w.shape[1]
    tm = _pick(1024, m)
    tn = _pick(512, n)
    if transposed:
        w_spec = pl.BlockSpec((tn, d), lambda i, j: (j, 0))
        o_spec = pl.BlockSpec((tn, tm), lambda i, j: (j, i))
        o_shape = jax.ShapeDtypeStruct((n, m), BF16)
    else:
        w_spec = pl.BlockSpec((d, tn), lambda i, j: (0, j))
        o_spec = pl.BlockSpec((tm, tn), lambda i, j: (i, j))
        o_shape = jax.ShapeDtypeStruct((m, n), BF16)
    return pl.pallas_call(
        functools.partial(_inproj_kernel, transposed=transposed),
        grid=(m // tm, n // tn),
        in_specs=[pl.BlockSpec((tm, d), lambda i, j: (i, 0)), w_spec],
        out_specs=o_spec,
        out_shape=o_shape,
        scratch_shapes=[pltpu.VMEM((tm, d), BF16)],
        compiler_params=_cparams(("parallel", "arbitrary")),
        name="in_proj_t" if transposed else "in_proj",
    )(x2, w)


LOG2E = math.log2(math.e)
ONES_ROWS = 16


def _t5_bucket_np(dist, num_buckets):
    max_exact = num_buckets // 2
    n_f = np.maximum(dist, 1).astype(np.float32)
    large = max_exact + (np.log(n_f / np.float32(max_exact))
                         / np.float32(math.log(REL_MAX_DISTANCE / max_exact))
                         * np.float32(num_buckets - max_exact)).astype(np.int32)
    large = np.minimum(large, num_buckets - 1)
    return np.where(dist < max_exact, dist, large).astype(np.int32)


def _bias_kernel(rb_ref, bucket_ref, o_ref, *, nb):
    h = pl.program_id(0)
    bucket = bucket_ref[...]
    tile = jnp.zeros(bucket.shape, F32)
    for bkt in range(nb):
        tile = jnp.where(bucket == bkt, rb_ref[bkt, h], tile)
    tile = (tile - rb_ref[nb - 1, h]) * LOG2E
    o_ref[0] = jnp.where(bucket < 0, -jnp.inf, tile)


def _bias_tiles(rel_bias, t):
    nb, n_heads = rel_bias.shape
    assert t + 1 >= REL_MAX_DISTANCE
    key = np.arange(t, dtype=np.int32)[:, None]
    qry = np.arange(t, dtype=np.int32)[None, :]
    d_diag = qry - key
    diag = np.where(d_diag >= 0, _t5_bucket_np(np.maximum(d_diag, 0), nb), -1)
    prev = _t5_bucket_np(d_diag + t, nb)
    buckets = jnp.asarray(np.stack([prev, diag]).astype(np.int32))
    return pl.pallas_call(
        functools.partial(_bias_kernel, nb=nb),
        grid=(n_heads,),
        in_specs=[pl.BlockSpec(memory_space=pltpu.SMEM),
                  pl.BlockSpec((2, t, t), lambda h: (0, 0, 0))],
        out_specs=pl.BlockSpec((1, 2, t, t), lambda h: (h, 0, 0, 0)),
        out_shape=jax.ShapeDtypeStruct((n_heads, 2, t, t), F32),
        compiler_params=_cparams(("arbitrary",)),
        name="rel_bias_tiles",
    )(rel_bias.astype(F32), buckets)


def _attn_kernel(lam_ref, g_ref, qt_ref, k_ref, vt_ref, bias_ref, o_ref, vext_ref,
                 *, t, lambda_init):
    qi = pl.program_id(2)
    dv = ATTN_VALUE_DIM

    @pl.when(qi == 0)
    def _():
        vext_ref[0:dv, :] = vt_ref[...]
        vext_ref[dv:, :] = jnp.ones((ONES_ROWS, vext_ref.shape[1]), BF16)

    qf = qt_ref[...].astype(F32) * (ATTN_HEAD_DIM ** -0.5 * LOG2E)
    row = lax.broadcasted_iota(jnp.int32, qf.shape, 0)
    comps = (jnp.where(row < ATTN_HEAD_DIM, qf, 0.0).astype(BF16),
             jnp.where(row >= ATTN_HEAD_DIM, qf, 0.0).astype(BF16))

    def qk(j):
        kj = k_ref[0, pl.ds(pl.multiple_of(j * t, t), t), :]
        return tuple(jnp.dot(kj, comps[c], preferred_element_type=F32) for c in range(2))

    def step(j, bias, state, prefetch):
        scores, chains = state
        nxt = qk(j + 1) if prefetch else scores
        vj = vext_ref[:, pl.ds(pl.multiple_of(j * t, t), t)]
        stats = []
        for c in range(2):
            s = scores[c] if bias is None else scores[c] + bias
            m_prev = chains[c][0]
            m_new = jnp.maximum(m_prev, jnp.max(s, axis=0, keepdims=True))
            stats.append((m_new, jnp.exp2(m_prev - m_new), jnp.exp2(s - m_new).astype(BF16)))
        return nxt, tuple((m_new, alpha * chains[c][1] + jnp.dot(vj, p, preferred_element_type=F32))
                          for c, (m_new, alpha, p) in enumerate(stats))

    init = (jnp.full((1, t), -jnp.inf, F32), jnp.zeros((dv + ONES_ROWS, t), F32))
    state = (qk(0), (init, init))
    state = lax.fori_loop(0, jnp.maximum(qi - 1, 0), lambda j, st: step(j, None, st, True), state)
    state = lax.cond(qi >= 1, lambda st: step(qi - 1, bias_ref[0, 0], st, True), lambda st: st, state)
    _, chains = step(qi, bias_ref[0, 1], state, False)

    lam_p = lam_ref[...]
    lam = (jnp.exp(jnp.sum(lam_p[0:1] * lam_p[1:2], axis=1, keepdims=True))
           - jnp.exp(jnp.sum(lam_p[2:3] * lam_p[3:4], axis=1, keepdims=True)) + lambda_init)
    a1 = chains[0][1]
    a2 = chains[1][1]
    o = a1[:dv] / a1[dv:dv + 1] - lam * (a2[:dv] / a2[dv:dv + 1])
    y = o * lax.rsqrt(jnp.mean(o * o, axis=0, keepdims=True) + LN_EPS)
    y = y * (g_ref[...] * (1.0 - lambda_init))
    o_ref[0] = y.T.astype(o_ref.dtype)


def _attention(qvt, k3, lam_params, subln_g_col, bias, n_heads, lambda_init):
    b, s, _ = k3.shape
    t = bias.shape[-1]
    dv = ATTN_VALUE_DIM
    nq = s // t
    kern = functools.partial(_attn_kernel, t=t, lambda_init=lambda_init)
    return pl.pallas_call(
        kern,
        grid=(b, n_heads, nq),
        in_specs=[pl.BlockSpec(lam_params.shape, lambda bi, h, qi: (0, 0)),
                  pl.BlockSpec((dv, 1), lambda bi, h, qi: (0, 0)),
                  pl.BlockSpec((dv, t), lambda bi, h, qi: (h, bi * nq + qi)),
                  pl.BlockSpec((1, s, dv), lambda bi, h, qi: (bi, 0, h)),
                  pl.BlockSpec((dv, s), lambda bi, h, qi: (n_heads + h, bi)),
                  pl.BlockSpec((1, 2, t, t), lambda bi, h, qi: (h, 0, 0, 0))],
        out_specs=pl.BlockSpec((1, t, dv), lambda bi, h, qi: (bi, qi, h)),
        out_shape=jax.ShapeDtypeStruct((b, s, n_heads * dv), BF16),
        scratch_shapes=[pltpu.VMEM((dv + ONES_ROWS, s), BF16)],
        compiler_params=_cparams(("parallel", "parallel", "arbitrary")),
        name="diff_attention",
    )(lam_params, subln_g_col, qvt, k3, qvt, bias)


CONV_HALO = 32
CONV_ROW_CHUNK = 64


def _conv_kernel(a_ref, g_ref, ah_ref, gh_ref, bglu_ref, w_ref, cb_ref, lng_ref, lnb_ref, o_ref,
                 h_ref, c_ref, *, ts, kw, ch):
    i = pl.program_id(1)
    ba = bglu_ref[:, :ch]
    bg = bglu_ref[:, ch:]

    def glu(a, g):
        return (a.astype(F32) + ba) * jax.nn.sigmoid(g.astype(F32) + bg)

    halo = glu(ah_ref[0], gh_ref[0])
    h_ref[0:CONV_HALO, :] = jnp.where(i > 0, halo, jnp.zeros_like(halo))
    h_ref[CONV_HALO:, :] = glu(a_ref[0], g_ref[0])

    rc = min(CONV_ROW_CHUNK, ts)
    for c in range(ch // LANES):
        cs = slice(c * LANES, (c + 1) * LANES)
        wc = w_ref[:, cs]
        cb = cb_ref[:, cs]

        for r0 in range(0, ts, rc):
            acc = jnp.zeros((rc, LANES), F32)
            for k in range(kw):
                off = r0 + CONV_HALO - (kw - 1) + k
                acc = acc + h_ref[off:off + rc, cs] * wc[k:k + 1, :]
            c_ref[r0:r0 + rc, cs] = acc + cb

    y = _ln_rows(c_ref[...], lng_ref[...], lnb_ref[...])
    o_ref[0] = (y * jax.nn.sigmoid(y)).astype(o_ref.dtype)


def _conformer_conv(proj3, b_glu, conv_w, conv_b, ln_g, ln_b, glu_col0):
    b, s, _ = proj3.shape
    kw, ch = conv_w.shape
    assert kw - 1 <= CONV_HALO and glu_col0 % ch == 0
    ts = _pick(256, s)
    cb0 = glu_col0 // ch
    hb = ts // CONV_HALO
    kern = functools.partial(_conv_kernel, ts=ts, kw=kw, ch=ch)
    halo_idx = lambda off: (lambda bi, i: (bi, jnp.maximum(i * hb - 1, 0), off))
    vec = lambda n: pl.BlockSpec((1, n), lambda bi, i: (0, 0))
    return pl.pallas_call(
        kern,
        grid=(b, s // ts),
        in_specs=[pl.BlockSpec((1, ts, ch), lambda bi, i: (bi, i, cb0)),
                  pl.BlockSpec((1, ts, ch), lambda bi, i: (bi, i, cb0 + 1)),
                  pl.BlockSpec((1, CONV_HALO, ch), halo_idx(cb0)),
                  pl.BlockSpec((1, CONV_HALO, ch), halo_idx(cb0 + 1)),
                  vec(2 * ch),
                  pl.BlockSpec((kw, ch), lambda bi, i: (0, 0)),
                  vec(ch), vec(ch), vec(ch)],
        out_specs=pl.BlockSpec((1, ts, ch), lambda bi, i: (bi, i, 0)),
        out_shape=jax.ShapeDtypeStruct((b, s, ch), BF16),
        scratch_shapes=[pltpu.VMEM((CONV_HALO + ts, ch), F32), pltpu.VMEM((ts, ch), F32)],
        compiler_params=_cparams(("parallel", "arbitrary")),
        name="conformer_conv",
    )(proj3, proj3, proj3, proj3, b_glu, conv_w, conv_b, ln_g, ln_b)


def _outproj_kernel(a_ref, c_ref, wa_ref, wc_ref, x_ref, g_ref, b_ref, o_ref, ob_ref):
    mix = (jnp.dot(a_ref[...], wa_ref[...], preferred_element_type=F32)
           + jnp.dot(c_ref[...], wc_ref[...], preferred_element_type=F32))
    y = _ln_rows(DEEPNORM_ALPHA * x_ref[...] + mix, g_ref[...], b_ref[...])
    o_ref[...] = y
    ob_ref[...] = y.astype(BF16)


def _out_proj_ln(attn2, conv2, w_out, x2, ln_g, ln_b):
    m, d = x2.shape
    da = attn2.shape[1]
    dc = conv2.shape[1]
    tm = _pick(512, m)
    vec = pl.BlockSpec((1, d), lambda i: (0, 0))
    return pl.pallas_call(
        _outproj_kernel,
        grid=(m // tm,),
        in_specs=[pl.BlockSpec((tm, da), lambda i: (i, 0)),
                  pl.BlockSpec((tm, dc), lambda i: (i, 0)),
                  pl.BlockSpec((da, d), lambda i: (0, 0)),
                  pl.BlockSpec((dc, d), lambda i: (0, 0)),
                  pl.BlockSpec((tm, d), lambda i: (i, 0)),
                  vec, vec],
        out_specs=[pl.BlockSpec((tm, d), lambda i: (i, 0)),
                   pl.BlockSpec((tm, d), lambda i: (i, 0))],
        out_shape=[jax.ShapeDtypeStruct((m, d), F32), jax.ShapeDtypeStruct((m, d), BF16)],
        compiler_params=_cparams(("parallel",)),
        name="out_proj_ln1",
    )(attn2, conv2, w_out[:da], w_out[da:], x2, ln_g, ln_b)


SUBLANES = 8
SELECT_UNROLL = 2


def _allmax8(x):
    for sh in (4, 2, 1):
        x = jnp.maximum(x, pltpu.roll(x, sh, 0))
    return x


def _allsum8(x):
    for sh in (4, 2, 1):
        x = x + pltpu.roll(x, sh, 0)
    return x


def _top_desc(s3, k):
    tops = []
    rank = jnp.full(s3.shape, float(k), F32)
    work = s3
    for a in range(k):
        m = _allmax8(jnp.max(work, axis=0))
        hit = work == m[None]
        rank = jnp.where(hit, float(a), rank)
        work = jnp.where(hit, -jnp.inf, work)
        tops.append(m)
    return tops, rank


def _kth_largest(c3, k):
    work = c3
    m = None
    for _ in range(k):
        m = _allmax8(jnp.max(work, axis=0))
        work = jnp.where(work == m[None], -jnp.inf, work)
    return m


def _rows_of(tops, lo):
    sub = lax.broadcasted_iota(jnp.int32, tops[lo].shape, 0)
    out = tops[lo]
    for r in range(1, SUBLANES):
        out = jnp.where(sub == r, tops[lo + r], out)
    return out


def _peer_select_kernel(x_ref, wq_ref, keys_ref, n_ref, ea_ref, rb_ref, eb_ref, s_ref,
                        *, n_heads, nk, tt):
    k = PEER_TOPK
    q = jnp.dot(x_ref[...], wq_ref[...], preferred_element_type=F32).astype(BF16)
    hd = keys_ref.shape[2]
    for hc in range(2 * n_heads):
        s_ref[hc] = lax.dot_general(keys_ref[hc], q[:, hc * hd:(hc + 1) * hd],
                                    (((1,), (1,)), ((), ())), preferred_element_type=F32)

    def group(h, c0):
        cols = pl.ds(c0, LANES)
        g3 = (nk // SUBLANES, SUBLANES, LANES)
        sa = s_ref[2 * h, :, cols].reshape(g3)
        sb = s_ref[2 * h + 1, :, cols].reshape(g3)
        t1, ra = _top_desc(sa, k)
        t2, rb = _top_desc(sb, k)
        s2_lo = _rows_of(t2, 0)
        cand = jnp.stack([t1[0] + s2_lo, t1[0] + _rows_of(t2, SUBLANES)]
                         + [t1[a] + s2_lo for a in range(1, SUBLANES)]
                         + [_rows_of(t1, SUBLANES) + t2[0]])
        tau = _kth_largest(cand, k)
        sel = cand >= tau[None]
        z = _allsum8(jnp.sum(jnp.where(sel, jnp.exp(cand - (t1[0] + t2[0])[None]), 0.0), axis=0))
        picked = jnp.where(sel, 1.0, 0.0)
        n_i = jnp.where((ra < float(k)) & (sa + t2[0][None] >= tau[None]), 1.0, 0.0)
        for a in range(SUBLANES):
            n_a = _allsum8(picked[0] + picked[1] if a == 0 else picked[a + 1])
            n_i = jnp.where(ra == float(a), n_a[None], n_i)
        n_ref[h, :, cols] = n_i.reshape(nk, LANES)
        ea_ref[h, :, cols] = (jnp.exp(sa - t1[0][None]) / z[None]).reshape(nk, LANES)
        rb_ref[h, :, cols] = rb.reshape(nk, LANES).astype(BF16)
        eb_ref[h, :, cols] = jnp.exp(sb - t2[0][None]).reshape(nk, LANES).astype(BF16)

    per_head = tt // (LANES * SELECT_UNROLL)

    def body(it, carry):
        h = it // per_head
        base = (it % per_head) * (LANES * SELECT_UNROLL)
        for u in range(SELECT_UNROLL):
            group(h, pl.multiple_of(base + u * LANES, LANES))
        return carry

    lax.fori_loop(0, n_heads * per_head, body, 0)


def _peer_select(x1b, wq, keys2):
    m, d = x1b.shape
    hc, nk, hd = keys2.shape
    n_heads = hc // 2
    assert PEER_TOPK == 2 * SUBLANES and nk % SUBLANES == 0
    tt = _pick(512, m)
    kern = functools.partial(_peer_select_kernel, n_heads=n_heads, nk=nk, tt=tt)
    out_spec = pl.BlockSpec((n_heads, nk, tt), lambda i: (0, 0, i))
    shp = lambda dt: jax.ShapeDtypeStruct((n_heads, nk, m), dt)
    return pl.pallas_call(
        kern,
        grid=(m // tt,),
        in_specs=[pl.BlockSpec((tt, d), lambda i: (i, 0)),
                  pl.BlockSpec(wq.shape, lambda i: (0, 0)),
                  pl.BlockSpec(keys2.shape, lambda i: (0, 0, 0))],
        out_specs=[out_spec, out_spec, out_spec, out_spec],
        out_shape=[shp(F32), shp(F32), shp(BF16), shp(BF16)],
        scratch_shapes=[pltpu.VMEM((hc, nk, tt), F32)],
        compiler_params=_cparams(("parallel",)),
        name="peer_select",
    )(x1b, wq, keys2)


PEER_STEP_BLOCKS = 8
PEER_SLAB_BLOCKS = 2


def _peer_dense_kernel(xb_ref, u_ref, vt_ref, n_ref, ea_ref, rb_ref, eb_ref, x_ref, g_ref, b_ref,
                       o_ref, acc_ref, z_ref, w_ref, *, n_heads, ib, nk):
    c = pl.program_id(1)
    n_chunks = pl.num_programs(1) - 1
    cur = c % 2

    @pl.when(c == 0)
    def _():
        acc_ref[...] = jnp.zeros(acc_ref.shape, F32)
        w_ref[1] = jnp.zeros(w_ref.shape[1:], BF16)

    def drain():
        acc_ref[...] += jnp.dot(vt_ref[...], w_ref[1 - cur], preferred_element_type=F32)

    @pl.when(c < n_chunks)
    def _():
        slab = PEER_SLAB_BLOCKS * nk
        for r0 in range(0, ib * nk, slab):
            z_ref[r0:r0 + slab, :] = lax.dot_general(
                u_ref[r0:r0 + slab, :], xb_ref[...], (((1,), (1,)), ((), ())),
                preferred_element_type=F32)
        drain()
        g0 = pl.multiple_of(c * ib, ib)
        n_rows = [n_ref[h, pl.ds(g0, ib), :] for h in range(n_heads)]
        ea_rows = [ea_ref[h, pl.ds(g0, ib), :] for h in range(n_heads)]
        for i in range(ib):
            r0 = i * nk
            z = z_ref[r0:r0 + nk, :]
            act = 0.5 * z * (1.0 + lax.erf(z * (2.0 ** -0.5)))
            w = jnp.zeros(z.shape, BF16)
            for h in range(n_heads):
                eb = eb_ref[h]
                hit = rb_ref[h] < n_rows[h][i:i + 1, :].astype(BF16)
                w = w + jnp.where(hit, eb, jnp.zeros_like(eb)) * ea_rows[h][i:i + 1, :].astype(BF16)
            w_ref[cur, r0:r0 + nk, :] = act.astype(BF16) * w

    @pl.when(c == n_chunks)
    def _():
        drain()
        y = DEEPNORM_ALPHA * x_ref[...] + acc_ref[...].T
        o_ref[...] = _ln_rows(y, g_ref[...], b_ref[...])


def _peer_dense(x1b, x1, u, vt, sel, ln_g, ln_b, nk):
    m, d = x1.shape
    ne = u.shape[0]
    n_i, ea, rb, eb = sel
    n_heads = n_i.shape[0]
    tt = _pick(512, m)
    ib = PEER_STEP_BLOCKS
    ec = ib * nk
    kern = functools.partial(_peer_dense_kernel, n_heads=n_heads, ib=ib, nk=nk)
    once = pl.Buffered(1)
    sel_spec = pl.BlockSpec((n_heads, nk, tt), lambda t, c: (0, 0, t), pipeline_mode=once)
    vec = pl.BlockSpec((1, d), lambda t, c: (0, 0))
    nc = ne // ec
    return pl.pallas_call(
        kern,
        grid=(m // tt, nc + 1),
        in_specs=[pl.BlockSpec((tt, d), lambda t, c: (t, 0)),
                  pl.BlockSpec((ec, d), lambda t, c: (jnp.minimum(c, nc - 1), 0)),
                  pl.BlockSpec((d, ec), lambda t, c: (0, jnp.maximum(c - 1, 0))),
                  sel_spec, sel_spec, sel_spec, sel_spec,
                  pl.BlockSpec((tt, d), lambda t, c: (t, 0), pipeline_mode=once),
                  vec, vec],
        out_specs=pl.BlockSpec((tt, d), lambda t, c: (t, 0)),
        out_shape=jax.ShapeDtypeStruct((m, d), F32),
        scratch_shapes=[pltpu.VMEM((d, tt), F32), pltpu.VMEM((ec, tt), F32),
                        pltpu.VMEM((2, ec, tt), BF16)],
        compiler_params=_cparams(("parallel", "arbitrary")),
        name="peer_experts_ln2",
    )(x1b, u, vt, n_i, ea, rb, eb, x1, ln_g, ln_b)


def kernel(x, w_in, b_glu, conv_w, conv_b, conv_ln_g, conv_ln_b, lambda_q1, lambda_k1, lambda_q2,
           lambda_k2, subln_g, rel_bias, w_out, ln1_g, ln1_b, peer_wq, peer_keys, peer_u, peer_v,
           ln2_g, ln2_b):
    b, s, d = x.shape
    n_layers = w_in.shape[0]
    n_heads = rel_bias.shape[1]
    qk_cols = n_heads * 2 * ATTN_HEAD_DIM
    v_cols = n_heads * ATTN_VALUE_DIM
    t_attn = _pick(512, s)
    bias = _bias_tiles(rel_bias, t_attn)
    row = lambda v: v.reshape(1, -1).astype(F32)

    x2 = x.reshape(b * s, d)
    for layer in range(n_layers):
        lambda_init = 0.8 - 0.6 * math.exp(-0.3 * layer)
        w = w_in[layer].astype(BF16)
        w_q, w_k, w_v, w_glu = (w[:, :qk_cols], w[:, qk_cols:2 * qk_cols],
                                w[:, 2 * qk_cols:2 * qk_cols + v_cols], w[:, 2 * qk_cols + v_cols:])
        kg3 = _in_proj(x2, jnp.concatenate([w_k, w_glu], axis=1), False).reshape(b, s, -1)
        qvt = _in_proj(x2, jnp.concatenate([w_q, w_v], axis=1).T, True)
        lam_params = jnp.stack([lambda_q1[layer], lambda_k1[layer],
                                lambda_q2[layer], lambda_k2[layer]]).astype(F32)
        attn = _attention(qvt, kg3, lam_params, subln_g[layer].reshape(-1, 1).astype(F32), bias,
                          n_heads, lambda_init)
        conv = _conformer_conv(kg3, row(b_glu[layer]), conv_w[layer].astype(F32),
                               row(conv_b[layer]), row(conv_ln_g[layer]), row(conv_ln_b[layer]),
                               qk_cols)
        x1, x1b = _out_proj_ln(attn.reshape(b * s, -1), conv.reshape(b * s, -1),
                               w_out[layer].astype(BF16), x2, row(ln1_g[layer]), row(ln1_b[layer]))
        keys = peer_keys[layer]
        ph, _, nk, hd = keys.shape
        sel = _peer_select(x1b, peer_wq[layer].astype(BF16),
                           keys.reshape(2 * ph, nk, hd).astype(BF16))
        x2 = _peer_dense(x1b, x1, peer_u[layer].astype(BF16), peer_v[layer].astype(BF16).T, sel,
                         row(ln2_g[layer]), row(ln2_b[layer]), nk)
    return x2.reshape(b, s, d)
```

```python
import functools
import math

import numpy as np

import jax
import jax.numpy as jnp
from jax import lax
from jax.experimental import pallas as pl
from jax.experimental.pallas import tpu as pltpu

F32 = jnp.float32
BF16 = jnp.bfloat16

ATTN_HEAD_DIM = 64
ATTN_VALUE_DIM = 2 * ATTN_HEAD_DIM
REL_MAX_DISTANCE = 128
PEER_TOPK = 16
LN_EPS = 1e-5
DEPTH = 1
DEEPNORM_ALPHA = (2.0 * DEPTH) ** 0.25

LANES = 128
VMEM_LIMIT_BYTES = 56 * 1024 * 1024


def _pick(limit, total, quantum=128):
    if total <= limit:
        return total
    t = (limit // quantum) * quantum
    while t > quantum and total % t:
        t -= quantum
    assert total % t == 0, (limit, total)
    return t


def _cparams(sem):
    return pltpu.CompilerParams(dimension_semantics=sem, vmem_limit_bytes=VMEM_LIMIT_BYTES)


def _ln_rows(y, g, b):
    mu = jnp.mean(y, axis=-1, keepdims=True)
    d = y - mu
    var = jnp.mean(d * d, axis=-1, keepdims=True)
    return d * lax.rsqrt(var + LN_EPS) * g + b


def _inproj_kernel(x_ref, w_ref, o_ref, xb_ref, *, transposed):
    @pl.when(pl.program_id(1) == 0)
    def _():
        xb_ref[...] = x_ref[...].astype(BF16)

    if transposed:
        o = lax.dot_general(w_ref[...], xb_ref[...], (((1,), (1,)), ((), ())),
                            preferred_element_type=F32)
    else:
        o = jnp.dot(xb_ref[...], w_ref[...], preferred_element_type=F32)
    o_ref[...] = o.astype(o_ref.dtype)


def _in_proj(x2, w, transposed):
    m, d = x2.shape
    n = w.shape[0] if transposed else w.shape[1]
    tm = _pick(1024, m)
    tn = _pick(512, n)
    if transposed:
        w_spec = pl.BlockSpec((tn, d), lambda i, j: (j, 0))
        o_spec = pl.BlockSpec((tn, tm), lambda i, j: (j, i))
        o_shape = jax.ShapeDtypeStruct((n, m), BF16)
    else:
        w_spec = pl.BlockSpec((d, tn), lambda i, j: (0, j))
        o_spec = pl.BlockSpec((tm, tn), lambda i, j: (i, j))
        o_shape = jax.ShapeDtypeStruct((m, n), BF16)
    return pl.pallas_call(
        functools.partial(_inproj_kernel, transposed=transposed),
        grid=(m // tm, n // tn),
        in_specs=[pl.BlockSpec((tm, d), lambda i, j: (i, 0)), w_spec],
        out_specs=o_spec,
        out_shape=o_shape,
        scratch_shapes=[pltpu.VMEM((tm, d), BF16)],
        compiler_params=_cparams(("parallel", "arbitrary")),
        name="in_proj_t" if transposed else "in_proj",
    )(x2, w)


LOG2E = math.log2(math.e)
ONES_ROWS = 16


def _t5_bucket_np(dist, num_buckets):
    max_exact = num_buckets // 2
    n_f = np.maximum(dist, 1).astype(np.float32)
    large = max_exact + (np.log(n_f / np.float32(max_exact))
                         / np.float32(math.log(REL_MAX_DISTANCE / max_exact))
                         * np.float32(num_buckets - max_exact)).astype(np.int32)
    large = np.minimum(large, num_buckets - 1)
    return np.where(dist < max_exact, dist, large).astype(np.int32)


def _bias_kernel(rb_ref, bucket_ref, o_ref, *, nb):
    h = pl.program_id(0)
    bucket = bucket_ref[...]
    tile = jnp.zeros(bucket.shape, F32)
    for bkt in range(nb):
        tile = jnp.where(bucket == bkt, rb_ref[bkt, h], tile)
    tile = (tile - rb_ref[nb - 1, h]) * LOG2E
    o_ref[0] = jnp.where(bucket < 0, -jnp.inf, tile)


def _bias_tiles(rel_bias, t):
    nb, n_heads = rel_bias.shape
    assert t + 1 >= REL_MAX_DISTANCE
    key = np.arange(t, dtype=np.int32)[:, None]
    qry = np.arange(t, dtype=np.int32)[None, :]
    d_diag = qry - key
    diag = np.where(d_diag >= 0, _t5_bucket_np(np.maximum(d_diag, 0), nb), -1)
    prev = _t5_bucket_np(d_diag + t, nb)
    buckets = jnp.asarray(np.stack([prev, diag]).astype(np.int32))
    return pl.pallas_call(
        functools.partial(_bias_kernel, nb=nb),
        grid=(n_heads,),
        in_specs=[pl.BlockSpec(memory_space=pltpu.SMEM),
                  pl.BlockSpec((2, t, t), lambda h: (0, 0, 0))],
        out_specs=pl.BlockSpec((1, 2, t, t), lambda h: (h, 0, 0, 0)),
        out_shape=jax.ShapeDtypeStruct((n_heads, 2, t, t), F32),
        compiler_params=_cparams(("arbitrary",)),
        name="rel_bias_tiles",
    )(rel_bias.astype(F32), buckets)


def _attn_kernel(lam_ref, g_ref, qt_ref, k_ref, vt_ref, bias_ref, o_ref, vext_ref,
                 *, t, lambda_init):
    qi = pl.program_id(2)
    dv = ATTN_VALUE_DIM

    @pl.when(qi == 0)
    def _():
        vext_ref[0:dv, :] = vt_ref[...]
        vext_ref[dv:, :] = jnp.ones((ONES_ROWS, vext_ref.shape[1]), BF16)

    qf = qt_ref[...].astype(F32) * (ATTN_HEAD_DIM ** -0.5 * LOG2E)
    row = lax.broadcasted_iota(jnp.int32, qf.shape, 0)
    comps = (jnp.where(row < ATTN_HEAD_DIM, qf, 0.0).astype(BF16),
             jnp.where(row >= ATTN_HEAD_DIM, qf, 0.0).astype(BF16))

    def qk(j):
        kj = k_ref[0, pl.ds(pl.multiple_of(j * t, t), t), :]
        return tuple(jnp.dot(kj, comps[c], preferred_element_type=F32) for c in range(2))

    def step(j, bias_idx, state, prefetch):
        scores, chains = state
        nxt = qk(j + 1) if prefetch else scores
        vj = vext_ref[:, pl.ds(pl.multiple_of(j * t, t), t)]

        def rows(c, r0, n):
            s = scores[c][r0:r0 + n, :]
            return s if bias_idx is None else s + bias_ref[0, bias_idx, r0:r0 + n, :]

        stats = []
        for c in range(2):
            m_prev = chains[c][0]
            m8 = rows(c, 0, SUBLANES)
            for r0 in range(SUBLANES, t, SUBLANES):
                m8 = jnp.maximum(m8, rows(c, r0, SUBLANES))
            m_new = jnp.maximum(m_prev, _allmax8(m8))
            m16 = jnp.concatenate([m_new, m_new], axis=0)
            p = jnp.concatenate([jnp.exp2(rows(c, r0, 2 * SUBLANES) - m16).astype(BF16)
                                 for r0 in range(0, t, 2 * SUBLANES)], axis=0)
            stats.append((m_new, jnp.exp2(m_prev - m_new), p))
        out = []
        for c, (m_new, alpha, p) in enumerate(stats):
            acc = chains[c][1]
            acc = (acc.reshape(-1, SUBLANES, t) * alpha[None]).reshape(acc.shape)
            out.append((m_new, acc + jnp.dot(vj, p, preferred_element_type=F32)))
        return nxt, tuple(out)

    init = (jnp.full((SUBLANES, t), -jnp.inf, F32), jnp.zeros((dv + ONES_ROWS, t), F32))
    state = (qk(0), (init, init))
    state = lax.fori_loop(0, jnp.maximum(qi - 1, 0), lambda j, st: step(j, None, st, True), state)
    state = lax.cond(qi >= 1, lambda st: step(qi - 1, 0, st, True), lambda st: st, state)
    _, chains = step(qi, 1, state, False)

    lam_p = lam_ref[...]
    lam = (jnp.exp(jnp.sum(lam_p[0:1] * lam_p[1:2], axis=1, keepdims=True))
           - jnp.exp(jnp.sum(lam_p[2:3] * lam_p[3:4], axis=1, keepdims=True)) + lambda_init)
    a1 = chains[0][1]
    a2 = chains[1][1]
    o = a1[:dv] / a1[dv:dv + 1] - lam * (a2[:dv] / a2[dv:dv + 1])
    y = o * lax.rsqrt(jnp.mean(o * o, axis=0, keepdims=True) + LN_EPS)
    y = y * (g_ref[...] * (1.0 - lambda_init))
    o_ref[0] = y.T.astype(o_ref.dtype)


def _attention(qvt, k3, lam_params, subln_g_col, bias, n_heads, lambda_init):
    b, s, _ = k3.shape
    t = bias.shape[-1]
    dv = ATTN_VALUE_DIM
    nq = s // t
    kern = functools.partial(_attn_kernel, t=t, lambda_init=lambda_init)
    return pl.pallas_call(
        kern,
        grid=(b, n_heads, nq),
        in_specs=[pl.BlockSpec(lam_params.shape, lambda bi, h, qi: (0, 0)),
                  pl.BlockSpec((dv, 1), lambda bi, h, qi: (0, 0)),
                  pl.BlockSpec((dv, t), lambda bi, h, qi: (h, bi * nq + qi)),
                  pl.BlockSpec((1, s, dv), lambda bi, h, qi: (bi, 0, h)),
                  pl.BlockSpec((dv, s), lambda bi, h, qi: (n_heads + h, bi)),
                  pl.BlockSpec((1, 2, t, t), lambda bi, h, qi: (h, 0, 0, 0))],
        out_specs=pl.BlockSpec((1, t, dv), lambda bi, h, qi: (bi, qi, h)),
        out_shape=jax.ShapeDtypeStruct((b, s, n_heads * dv), BF16),
        scratch_shapes=[pltpu.VMEM((dv + ONES_ROWS, s), BF16)],
        compiler_params=_cparams(("parallel", "parallel", "arbitrary")),
        name="diff_attention",
    )(lam_params, subln_g_col, qvt, k3, qvt, bias)


CONV_HALO = 32
CONV_ROW_CHUNK = 64


def _conv_kernel(a_ref, g_ref, ah_ref, gh_ref, bglu_ref, w_ref, cb_ref, lng_ref, lnb_ref, o_ref,
                 h_ref, c_ref, *, ts, kw, ch):
    i = pl.program_id(1)
    ba = bglu_ref[:, :ch]
    bg = bglu_ref[:, ch:]

    def glu(a, g):
        return (a.astype(F32) + ba) * jax.nn.sigmoid(g.astype(F32) + bg)

    halo = glu(ah_ref[0], gh_ref[0])
    h_ref[0:CONV_HALO, :] = jnp.where(i > 0, halo, jnp.zeros_like(halo))
    h_ref[CONV_HALO:, :] = glu(a_ref[0], g_ref[0])

    rc = min(CONV_ROW_CHUNK, ts)
    for c in range(ch // LANES):
        cs = slice(c * LANES, (c + 1) * LANES)
        wc = w_ref[:, cs]
        cb = cb_ref[:, cs]

        for r0 in range(0, ts, rc):
            acc = jnp.zeros((rc, LANES), F32)
            for k in range(kw):
                off = r0 + CONV_HALO - (kw - 1) + k
                acc = acc + h_ref[off:off + rc, cs] * wc[k:k + 1, :]
            c_ref[r0:r0 + rc, cs] = acc + cb

    y = _ln_rows(c_ref[...], lng_ref[...], lnb_ref[...])
    o_ref[0] = (y * jax.nn.sigmoid(y)).astype(o_ref.dtype)


def _conformer_conv(proj3, b_glu, conv_w, conv_b, ln_g, ln_b, glu_col0):
    b, s, _ = proj3.shape
    kw, ch = conv_w.shape
    assert kw - 1 <= CONV_HALO and glu_col0 % ch == 0
    ts = _pick(256, s)
    cb0 = glu_col0 // ch
    hb = ts // CONV_HALO
    kern = functools.partial(_conv_kernel, ts=ts, kw=kw, ch=ch)
    halo_idx = lambda off: (lambda bi, i: (bi, jnp.maximum(i * hb - 1, 0), off))
    vec = lambda n: pl.BlockSpec((1, n), lambda bi, i: (0, 0))
    return pl.pallas_call(
        kern,
        grid=(b, s // ts),
        in_specs=[pl.BlockSpec((1, ts, ch), lambda bi, i: (bi, i, cb0)),
                  pl.BlockSpec((1, ts, ch), lambda bi, i: (bi, i, cb0 + 1)),
                  pl.BlockSpec((1, CONV_HALO, ch), halo_idx(cb0)),
                  pl.BlockSpec((1, CONV_HALO, ch), halo_idx(cb0 + 1)),
                  vec(2 * ch),
                  pl.BlockSpec((kw, ch), lambda bi, i: (0, 0)),
                  vec(ch), vec(ch), vec(ch)],
        out_specs=pl.BlockSpec((1, ts, ch), lambda bi, i: (bi, i, 0)),
        out_shape=jax.ShapeDtypeStruct((b, s, ch), BF16),
        scratch_shapes=[pltpu.VMEM((CONV_HALO + ts, ch), F32), pltpu.VMEM((ts, ch), F32)],
        compiler_params=_cparams(("parallel", "arbitrary")),
        name="conformer_conv",
    )(proj3, proj3, proj3, proj3, b_glu, conv_w, conv_b, ln_g, ln_b)


def _outproj_kernel(a_ref, c_ref, wa_ref, wc_ref, x_ref, g_ref, b_ref, o_ref, ob_ref):
    mix = (jnp.dot(a_ref[...], wa_ref[...], preferred_element_type=F32)
           + jnp.dot(c_ref[...], wc_ref[...], preferred_element_type=F32))
    y = _ln_rows(DEEPNORM_ALPHA * x_ref[...] + mix, g_ref[...], b_ref[...])
    o_ref[...] = y
    ob_ref[...] = y.astype(BF16)


def _out_proj_ln(attn2, conv2, w_out, x2, ln_g, ln_b):
    m, d = x2.shape
    da = attn2.shape[1]
    dc = conv2.shape[1]
    tm = _pick(512, m)
    vec = pl.BlockSpec((1, d), lambda i: (0, 0))
    return pl.pallas_call(
        _outproj_kernel,
        grid=(m // tm,),
        in_specs=[pl.BlockSpec((tm, da), lambda i: (i, 0)),
                  pl.BlockSpec((tm, dc), lambda i: (i, 0)),
                  pl.BlockSpec((da, d), lambda i: (0, 0)),
                  pl.BlockSpec((dc, d), lambda i: (0, 0)),
                  pl.BlockSpec((tm, d), lambda i: (i, 0)),
                  vec, vec],
        out_specs=[pl.BlockSpec((tm, d), lambda i: (i, 0)),
                   pl.BlockSpec((tm, d), lambda i: (i, 0))],
        out_shape=[jax.ShapeDtypeStruct((m, d), F32), jax.ShapeDtypeStruct((m, d), BF16)],
        compiler_params=_cparams(("parallel",)),
        name="out_proj_ln1",
    )(attn2, conv2, w_out[:da], w_out[da:], x2, ln_g, ln_b)


SUBLANES = 8
SELECT_UNROLL = 2


def _allmax8(x):
    for sh in (4, 2, 1):
        x = jnp.maximum(x, pltpu.roll(x, sh, 0))
    return x


def _allsum8(x):
    for sh in (4, 2, 1):
        x = x + pltpu.roll(x, sh, 0)
    return x


def _top_desc(s3, k):
    tops = []
    rank = jnp.full(s3.shape, float(k), F32)
    work = s3
    for a in range(k):
        m = _allmax8(jnp.max(work, axis=0))
        hit = work == m[None]
        rank = jnp.where(hit, float(a), rank)
        work = jnp.where(hit, -jnp.inf, work)
        tops.append(m)
    return tops, rank


def _kth_largest(c3, k):
    work = c3
    m = None
    for _ in range(k):
        m = _allmax8(jnp.max(work, axis=0))
        work = jnp.where(work == m[None], -jnp.inf, work)
    return m


def _rows_of(tops, lo):
    sub = lax.broadcasted_iota(jnp.int32, tops[lo].shape, 0)
    out = tops[lo]
    for r in range(1, SUBLANES):
        out = jnp.where(sub == r, tops[lo + r], out)
    return out


def _peer_select_kernel(x_ref, wq_ref, keys_ref, n_ref, ea_ref, rb_ref, eb_ref, s_ref,
                        *, n_heads, nk, tt):
    k = PEER_TOPK
    q = jnp.dot(x_ref[...], wq_ref[...], preferred_element_type=F32).astype(BF16)
    hd = keys_ref.shape[2]
    for hc in range(2 * n_heads):
        s_ref[hc] = lax.dot_general(keys_ref[hc], q[:, hc * hd:(hc + 1) * hd],
                                    (((1,), (1,)), ((), ())), preferred_element_type=F32)

    def group(h, c0):
        cols = pl.ds(c0, LANES)
        g3 = (nk // SUBLANES, SUBLANES, LANES)
        sa = s_ref[2 * h, :, cols].reshape(g3)
        sb = s_ref[2 * h + 1, :, cols].reshape(g3)
        t1, ra = _top_desc(sa, k)
        t2, rb = _top_desc(sb, k)
        s2_lo = _rows_of(t2, 0)
        cand = jnp.stack([t1[0] + s2_lo, t1[0] + _rows_of(t2, SUBLANES)]
                         + [t1[a] + s2_lo for a in range(1, SUBLANES)]
                         + [_rows_of(t1, SUBLANES) + t2[0]])
        tau = _kth_largest(cand, k)
        sel = cand >= tau[None]
        z = _allsum8(jnp.sum(jnp.where(sel, jnp.exp(cand - (t1[0] + t2[0])[None]), 0.0), axis=0))
        picked = jnp.where(sel, 1.0, 0.0)
        n_i = jnp.where((ra < float(k)) & (sa + t2[0][None] >= tau[None]), 1.0, 0.0)
        for a in range(SUBLANES):
            n_a = _allsum8(picked[0] + picked[1] if a == 0 else picked[a + 1])
            n_i = jnp.where(ra == float(a), n_a[None], n_i)
        n_ref[h, :, cols] = n_i.reshape(nk, LANES)
        ea_ref[h, :, cols] = (jnp.exp(sa - t1[0][None]) / z[None]).reshape(nk, LANES)
        rb_ref[h, :, cols] = rb.reshape(nk, LANES).astype(BF16)
        eb_ref[h, :, cols] = jnp.exp(sb - t2[0][None]).reshape(nk, LANES).astype(BF16)

    per_head = tt // (LANES * SELECT_UNROLL)

    def body(it, carry):
        h = it // per_head
        base = (it % per_head) * (LANES * SELECT_UNROLL)
        for u in range(SELECT_UNROLL):
            group(h, pl.multiple_of(base + u * LANES, LANES))
        return carry

    lax.fori_loop(0, n_heads * per_head, body, 0)


def _peer_select(x1b, wq, keys2):
    m, d = x1b.shape
    hc, nk, hd = keys2.shape
    n_heads = hc // 2
    assert PEER_TOPK == 2 * SUBLANES and nk % SUBLANES == 0
    tt = _pick(512, m)
    kern = functools.partial(_peer_select_kernel, n_heads=n_heads, nk=nk, tt=tt)
    out_spec = pl.BlockSpec((n_heads, nk, tt), lambda i: (0, 0, i))
    shp = lambda dt: jax.ShapeDtypeStruct((n_heads, nk, m), dt)
    return pl.pallas_call(
        kern,
        grid=(m // tt,),
        in_specs=[pl.BlockSpec((tt, d), lambda i: (i, 0)),
                  pl.BlockSpec(wq.shape, lambda i: (0, 0)),
                  pl.BlockSpec(keys2.shape, lambda i: (0, 0, 0))],
        out_specs=[out_spec, out_spec, out_spec, out_spec],
        out_shape=[shp(F32), shp(F32), shp(BF16), shp(BF16)],
        scratch_shapes=[pltpu.VMEM((hc, nk, tt), F32)],
        compiler_params=_cparams(("parallel",)),
        name="peer_select",
    )(x1b, wq, keys2)


PEER_STEP_BLOCKS = 8
PEER_SLAB_BLOCKS = 2


def _peer_dense_kernel(xb_ref, u_ref, vt_ref, n_ref, ea_ref, rb_ref, eb_ref, x_ref, g_ref, b_ref,
                       o_ref, acc_ref, z_ref, w_ref, *, n_heads, ib, nk):
    c = pl.program_id(1)
    n_chunks = pl.num_programs(1) - 1
    cur = c % 2

    @pl.when(c == 0)
    def _():
        acc_ref[...] = jnp.zeros(acc_ref.shape, F32)
        w_ref[1] = jnp.zeros(w_ref.shape[1:], BF16)

    def drain():
        acc_ref[...] += jnp.dot(vt_ref[0], w_ref[1 - cur], preferred_element_type=F32)

    @pl.when(c < n_chunks)
    def _():
        slab = PEER_SLAB_BLOCKS * nk
        for r0 in range(0, ib * nk, slab):
            z_ref[r0:r0 + slab, :] = lax.dot_general(
                u_ref[r0:r0 + slab, :], xb_ref[...], (((1,), (1,)), ((), ())),
                preferred_element_type=F32)
        drain()
        g0 = pl.multiple_of(c * ib, ib)
        n_rows = [n_ref[h, pl.ds(g0, ib), :] for h in range(n_heads)]
        ea_rows = [ea_ref[h, pl.ds(g0, ib), :] for h in range(n_heads)]
        for i in range(ib):
            r0 = i * nk
            z = z_ref[r0:r0 + nk, :]
            act = 0.5 * z * (1.0 + lax.erf(z * (2.0 ** -0.5)))
            w = jnp.zeros(z.shape, BF16)
            for h in range(n_heads):
                eb = eb_ref[h]
                hit = rb_ref[h] < n_rows[h][i:i + 1, :].astype(BF16)
                w = w + jnp.where(hit, eb, jnp.zeros_like(eb)) * ea_rows[h][i:i + 1, :].astype(BF16)
            w_ref[cur, r0:r0 + nk, :] = act.astype(BF16) * w

    @pl.when(c == n_chunks)
    def _():
        drain()
        y = DEEPNORM_ALPHA * x_ref[...] + acc_ref[...].T
        o_ref[...] = _ln_rows(y, g_ref[...], b_ref[...])


def _peer_dense(x1b, x1, u, v, sel, ln_g, ln_b, nk):
    m, d = x1.shape
    ne = u.shape[0]
    n_i, ea, rb, eb = sel
    n_heads = n_i.shape[0]
    tt = _pick(512, m)
    ib = PEER_STEP_BLOCKS
    ec = ib * nk
    vt = v.reshape(ne // ec, ec, d).transpose(0, 2, 1)
    kern = functools.partial(_peer_dense_kernel, n_heads=n_heads, ib=ib, nk=nk)
    once = pl.Buffered(1)
    sel_spec = pl.BlockSpec((n_heads, nk, tt), lambda t, c: (0, 0, t), pipeline_mode=once)
    vec = pl.BlockSpec((1, d), lambda t, c: (0, 0))
    nc = ne // ec
    return pl.pallas_call(
        kern,
        grid=(m // tt, nc + 1),
        in_specs=[pl.BlockSpec((tt, d), lambda t, c: (t, 0)),
                  pl.BlockSpec((ec, d), lambda t, c: (jnp.minimum(c, nc - 1), 0)),
                  pl.BlockSpec((1, d, ec), lambda t, c: (jnp.maximum(c - 1, 0), 0, 0)),
                  sel_spec, sel_spec, sel_spec, sel_spec,
                  pl.BlockSpec((tt, d), lambda t, c: (t, 0), pipeline_mode=once),
                  vec, vec],
        out_specs=pl.BlockSpec((tt, d), lambda t, c: (t, 0)),
        out_shape=jax.ShapeDtypeStruct((m, d), F32),
        scratch_shapes=[pltpu.VMEM((d, tt), F32), pltpu.VMEM((ec, tt), F32),
                        pltpu.VMEM((2, ec, tt), BF16)],
        compiler_params=_cparams(("parallel", "arbitrary")),
        name="peer_experts_ln2",
    )(x1b, u, vt, n_i, ea, rb, eb, x1, ln_g, ln_b)


def kernel(x, w_in, b_glu, conv_w, conv_b, conv_ln_g, conv_ln_b, lambda_q1, lambda_k1, lambda_q2,
           lambda_k2, subln_g, rel_bias, w_out, ln1_g, ln1_b, peer_wq, peer_keys, peer_u, peer_v,
           ln2_g, ln2_b):
    b, s, d = x.shape
    n_layers = w_in.shape[0]
    n_heads = rel_bias.shape[1]
    qk_cols = n_heads * 2 * ATTN_HEAD_DIM
    v_cols = n_heads * ATTN_VALUE_DIM
    t_attn = _pick(512, s)
    bias = _bias_tiles(rel_bias, t_attn)
    row = lambda v: v.reshape(1, -1).astype(F32)

    x2 = x.reshape(b * s, d)
    for layer in range(n_layers):
        lambda_init = 0.8 - 0.6 * math.exp(-0.3 * layer)
        w = w_in[layer].astype(BF16)
        w_q, w_k, w_v, w_glu = (w[:, :qk_cols], w[:, qk_cols:2 * qk_cols],
                                w[:, 2 * qk_cols:2 * qk_cols + v_cols], w[:, 2 * qk_cols + v_cols:])
        kg3 = _in_proj(x2, jnp.concatenate([w_k, w_glu], axis=1), False).reshape(b, s, -1)
        qvt = _in_proj(x2, jnp.concatenate([w_q, w_v], axis=1).T, True)
        lam_params = jnp.stack([lambda_q1[layer], lambda_k1[layer],
                                lambda_q2[layer], lambda_k2[layer]]).astype(F32)
        attn = _attention(qvt, kg3, lam_params, subln_g[layer].reshape(-1, 1).astype(F32), bias,
                          n_heads, lambda_init)
        conv = _conformer_conv(kg3, row(b_glu[layer]), conv_w[layer].astype(F32),
                               row(conv_b[layer]), row(conv_ln_g[layer]), row(conv_ln_b[layer]),
                               qk_cols)
        x1, x1b = _out_proj_ln(attn.reshape(b * s, -1), conv.reshape(b * s, -1),
                               w_out[layer].astype(BF16), x2, row(ln1_g[layer]), row(ln1_b[layer]))
        keys = peer_keys[layer]
        ph, _, nk, hd = keys.shape
        sel = _peer_select(x1b, peer_wq[layer].astype(BF16),
                           keys.reshape(2 * ph, nk, hd).astype(BF16))
        x2 = _peer_dense(x1b, x1, peer_u[layer].astype(BF16), peer_v[layer].astype(BF16), sel,
                         row(ln2_g[layer]), row(ln2_b[layer]), nk)
    return x2.reshape(b, s, d)
```

```python
import functools
import math

import numpy as np

import jax
import jax.numpy as jnp
from jax import lax
from jax.experimental import pallas as pl
from jax.experimental.pallas import tpu as pltpu

F32 = jnp.float32
BF16 = jnp.bfloat16

ATTN_HEAD_DIM = 64
ATTN_VALUE_DIM = 2 * ATTN_HEAD_DIM
REL_MAX_DISTANCE = 128
PEER_TOPK = 16
LN_EPS = 1e-5
DEPTH = 1
DEEPNORM_ALPHA = (2.0 * DEPTH) ** 0.25

LANES = 128
SUBLANES = 8
VMEM_LIMIT_BYTES = 56 * 1024 * 1024


def _pick(limit, total, quantum=128):
    if total <= limit:
        return total
    t = (limit // quantum) * quantum
    while t > quantum and total % t:
        t -= quantum
    assert total % t == 0, (limit, total)
    return t


def _cparams(sem):
    return pltpu.CompilerParams(dimension_semantics=sem, vmem_limit_bytes=VMEM_LIMIT_BYTES)


def _allmax8(x):
    for sh in (4, 2, 1):
        x = jnp.maximum(x, pltpu.roll(x, sh, 0))
    return x


def _allsum8(x):
    for sh in (4, 2, 1):
        x = x + pltpu.roll(x, sh, 0)
    return x


def _ln_rows(y, g, b):
    mu = jnp.mean(y, axis=-1, keepdims=True)
    d = y - mu
    var = jnp.mean(d * d, axis=-1, keepdims=True)
    return d * lax.rsqrt(var + LN_EPS) * g + b


def _inproj_kernel(x_ref, w_ref, o_ref, xb_ref, *, transposed):
    @pl.when(pl.program_id(1) == 0)
    def _():
        xb_ref[...] = x_ref[...].astype(BF16)

    if transposed:
        o = lax.dot_general(w_ref[...], xb_ref[...], (((1,), (1,)), ((), ())),
                            preferred_element_type=F32)
    else:
        o = jnp.dot(xb_ref[...], w_ref[...], preferred_element_type=F32)
    o_ref[...] = o.astype(o_ref.dtype)


def _in_proj(x2, w, transposed):
    m, d = x2.shape
    n = w.shape[0] if transposed else w.shape[1]
    tm = _pick(1024, m)
    tn = _pick(1024, n)
    if transposed:
        w_spec = pl.BlockSpec((tn, d), lambda i, j: (j, 0))
        o_spec = pl.BlockSpec((tn, tm), lambda i, j: (j, i))
        o_shape = jax.ShapeDtypeStruct((n, m), BF16)
    else:
        w_spec = pl.BlockSpec((d, tn), lambda i, j: (0, j))
        o_spec = pl.BlockSpec((tm, tn), lambda i, j: (i, j))
        o_shape = jax.ShapeDtypeStruct((m, n), BF16)
    return pl.pallas_call(
        functools.partial(_inproj_kernel, transposed=transposed),
        grid=(m // tm, n // tn),
        in_specs=[pl.BlockSpec((tm, d), lambda i, j: (i, 0)), w_spec],
        out_specs=o_spec,
        out_shape=o_shape,
        scratch_shapes=[pltpu.VMEM((tm, d), BF16)],
        compiler_params=_cparams(("parallel", "arbitrary")),
        name="in_proj_t" if transposed else "in_proj",
    )(x2, w)


LOG2E = math.log2(math.e)
ONES_ROWS = 16


def _t5_bucket_np(dist, num_buckets):
    max_exact = num_buckets // 2
    n_f = np.maximum(dist, 1).astype(np.float32)
    large = max_exact + (np.log(n_f / np.float32(max_exact))
                         / np.float32(math.log(REL_MAX_DISTANCE / max_exact))
                         * np.float32(num_buckets - max_exact)).astype(np.int32)
    large = np.minimum(large, num_buckets - 1)
    return np.where(dist < max_exact, dist, large).astype(np.int32)


def _bias_kernel(rb_ref, bucket_ref, o_ref, *, nb):
    h = pl.program_id(0)
    bucket = bucket_ref[...]
    tile = jnp.zeros(bucket.shape, F32)
    for bkt in range(nb):
        tile = jnp.where(bucket == bkt, rb_ref[bkt, h], tile)
    tile = (tile - rb_ref[nb - 1, h]) * LOG2E
    o_ref[0] = jnp.where(bucket < 0, -jnp.inf, tile)


def _bias_tiles(rel_bias, t):
    nb, n_heads = rel_bias.shape
    assert t + 1 >= REL_MAX_DISTANCE
    key = np.arange(t, dtype=np.int32)[:, None]
    qry = np.arange(t, dtype=np.int32)[None, :]
    d_diag = qry - key
    diag = np.where(d_diag >= 0, _t5_bucket_np(np.maximum(d_diag, 0), nb), -1)
    prev = _t5_bucket_np(d_diag + t, nb)
    buckets = jnp.asarray(np.stack([prev, diag]).astype(np.int32))
    return pl.pallas_call(
        functools.partial(_bias_kernel, nb=nb),
        grid=(n_heads,),
        in_specs=[pl.BlockSpec(memory_space=pltpu.SMEM),
                  pl.BlockSpec((2, t, t), lambda h: (0, 0, 0))],
        out_specs=pl.BlockSpec((1, 2, t, t), lambda h: (h, 0, 0, 0)),
        out_shape=jax.ShapeDtypeStruct((n_heads, 2, t, t), F32),
        compiler_params=_cparams(("arbitrary",)),
        name="rel_bias_tiles",
    )(rel_bias.astype(F32), buckets)


def _attn_kernel(lam_ref, g_ref, qt_ref, k_ref, vt_ref, bias_ref, o_ref, vext_ref, sa_ref, sb_ref,
                    pa_ref, pb_ref, acc_ref, *, t, lambda_init):
    qi = pl.program_id(2)
    dv = ATTN_VALUE_DIM

    @pl.when(qi == 0)
    def _():
        vext_ref[0:dv, :] = vt_ref[...]
        vext_ref[dv:, :] = jnp.ones((ONES_ROWS, vext_ref.shape[1]), BF16)

    qf = qt_ref[...].astype(F32) * (ATTN_HEAD_DIM ** -0.5 * LOG2E)
    row = lax.broadcasted_iota(jnp.int32, qf.shape, 0)
    comps = (jnp.where(row < ATTN_HEAD_DIM, qf, 0.0).astype(BF16),
             jnp.where(row >= ATTN_HEAD_DIM, qf, 0.0).astype(BF16))
    s_bufs = (sa_ref, sb_ref)
    p_bufs = (pa_ref, pb_ref)

    def qk(j, buf):
        kj = k_ref[0, pl.ds(pl.multiple_of(j * t, t), t), :]
        for c in range(2):
            s_bufs[buf][c] = jnp.dot(kj, comps[c], preferred_element_type=F32)

    def step(j, buf, bias_idx, ms, prefetch):
        if prefetch:
            qk(j + 1, 1 - buf)
        s_ref, p_ref = s_bufs[buf], p_bufs[buf]
        vj = vext_ref[:, pl.ds(pl.multiple_of(j * t, t), t)]

        def rows(c, r0, n):
            s = s_ref[c, r0:r0 + n, :]
            return s if bias_idx is None else s + bias_ref[0, bias_idx, r0:r0 + n, :]

        out = []
        for c in range(2):
            m8 = rows(c, 0, SUBLANES)
            for r0 in range(SUBLANES, t, SUBLANES):
                m8 = jnp.maximum(m8, rows(c, r0, SUBLANES))
            m_new = jnp.maximum(ms[c], _allmax8(m8))
            alpha = jnp.exp2(ms[c] - m_new)
            m16 = jnp.concatenate([m_new, m_new], axis=0)
            for r0 in range(0, t, 2 * SUBLANES):
                p_ref[c, r0:r0 + 2 * SUBLANES, :] = jnp.exp2(rows(c, r0, 2 * SUBLANES) - m16).astype(BF16)
            out.append((m_new, alpha))
        for c in range(2):
            acc3 = acc_ref[c].reshape(-1, SUBLANES, t) * out[c][1][None]
            acc_ref[c] = acc3.reshape(acc_ref.shape[1:]) + jnp.dot(vj, p_ref[c], preferred_element_type=F32)
        return tuple(m for m, _ in out)

    acc_ref[...] = jnp.zeros(acc_ref.shape, F32)
    m0 = jnp.full((SUBLANES, t), -jnp.inf, F32)
    qk(0, 0)
    n_far = jnp.maximum(qi - 1, 0)

    def pair(i, ms):
        ms = step(2 * i, 0, None, ms, True)
        return step(2 * i + 1, 1, None, ms, True)

    ms = lax.fori_loop(0, n_far // 2, pair, (m0, m0))

    def tail_first(ms):
        return step(qi, 0, 1, ms, False)

    def tail_even(ms):
        ms = step(qi - 1, 0, 0, ms, True)
        return step(qi, 1, 1, ms, False)

    def tail_odd(ms):
        ms = step(qi - 2, 0, None, ms, True)
        ms = step(qi - 1, 1, 0, ms, True)
        return step(qi, 0, 1, ms, False)

    branch = jnp.where(qi == 0, 0, 1 + n_far % 2)
    lax.switch(branch, [tail_first, tail_even, tail_odd], ms)

    lam_p = lam_ref[...]
    lam = (jnp.exp(jnp.sum(lam_p[0:1] * lam_p[1:2], axis=1, keepdims=True))
           - jnp.exp(jnp.sum(lam_p[2:3] * lam_p[3:4], axis=1, keepdims=True)) + lambda_init)
    a1 = acc_ref[0]
    a2 = acc_ref[1]
    o = a1[:dv] / a1[dv:dv + 1] - lam * (a2[:dv] / a2[dv:dv + 1])
    y = o * lax.rsqrt(jnp.mean(o * o, axis=0, keepdims=True) + LN_EPS)
    y = y * (g_ref[...] * (1.0 - lambda_init))
    o_ref[0] = y.T.astype(o_ref.dtype)


def _attention(qvt, k3, lam_params, subln_g_col, bias, n_heads, lambda_init):
    b, s, _ = k3.shape
    t = bias.shape[-1]
    dv = ATTN_VALUE_DIM
    nq = s // t
    kern = functools.partial(_attn_kernel, t=t, lambda_init=lambda_init)
    return pl.pallas_call(
        kern,
        grid=(b, n_heads, nq),
        in_specs=[pl.BlockSpec(lam_params.shape, lambda bi, h, qi: (0, 0)),
                  pl.BlockSpec((dv, 1), lambda bi, h, qi: (0, 0)),
                  pl.BlockSpec((dv, t), lambda bi, h, qi: (h, bi * nq + qi)),
                  pl.BlockSpec((1, s, dv), lambda bi, h, qi: (bi, 0, h)),
                  pl.BlockSpec((dv, s), lambda bi, h, qi: (n_heads + h, bi)),
                  pl.BlockSpec((1, 2, t, t), lambda bi, h, qi: (h, 0, 0, 0))],
        out_specs=pl.BlockSpec((1, t, dv), lambda bi, h, qi: (bi, qi, h)),
        out_shape=jax.ShapeDtypeStruct((b, s, n_heads * dv), BF16),
        scratch_shapes=[pltpu.VMEM((dv + ONES_ROWS, s), BF16),
                        pltpu.VMEM((2, t, t), F32), pltpu.VMEM((2, t, t), F32),
                        pltpu.VMEM((2, t, t), BF16), pltpu.VMEM((2, t, t), BF16),
                        pltpu.VMEM((2, dv + ONES_ROWS, t), F32)],
        compiler_params=_cparams(("parallel", "parallel", "arbitrary")),
        name="diff_attention",
    )(lam_params, subln_g_col, qvt, k3, qvt, bias)


CONV_HALO = 32
CONV_ROW_CHUNK = 64


def _conv_kernel(a_ref, g_ref, ah_ref, gh_ref, bglu_ref, w_ref, cb_ref, lng_ref, lnb_ref, o_ref,
                 h_ref, c_ref, *, ts, kw, ch):
    i = pl.program_id(1)
    ba = bglu_ref[:, :ch]
    bg = bglu_ref[:, ch:]

    def glu(a, g):
        return (a.astype(F32) + ba) * jax.nn.sigmoid(g.astype(F32) + bg)

    halo = glu(ah_ref[0], gh_ref[0])
    h_ref[0, 0:CONV_HALO, :] = jnp.where(i > 0, halo, jnp.zeros_like(halo))
    h_ref[0, CONV_HALO:, :] = glu(a_ref[0], g_ref[0])
    lead = CONV_HALO - (kw - 1)
    span = ts + ((lead + kw - 1) // SUBLANES) * SUBLANES
    for b in range(1, SUBLANES):
        n = min(span, CONV_HALO + ts - b)
        h_ref[b, 0:n, :] = h_ref[0, b:b + n, :]

    rc = min(CONV_ROW_CHUNK, ts)
    for c in range(ch // LANES):
        cs = slice(c * LANES, (c + 1) * LANES)
        wc = w_ref[:, cs]
        cb = cb_ref[:, cs]

        accs = [jnp.zeros((rc // SUBLANES, SUBLANES, LANES), F32) for _ in range(0, ts, rc)]
        for k in range(kw):
            a, b = divmod(lead + k, SUBLANES)
            w8 = jnp.broadcast_to(wc[k:k + 1, :], (SUBLANES, LANES))[None]
            for ri, r0 in enumerate(range(0, ts, rc)):
                off = r0 + a * SUBLANES
                accs[ri] = accs[ri] + h_ref[b, off:off + rc, cs].reshape(accs[ri].shape) * w8
        for ri, r0 in enumerate(range(0, ts, rc)):
            c_ref[r0:r0 + rc, cs] = accs[ri].reshape(rc, LANES) + cb

    y = _ln_rows(c_ref[...], lng_ref[...], lnb_ref[...])
    o_ref[0] = (y * jax.nn.sigmoid(y)).astype(o_ref.dtype)


def _conformer_conv(proj3, b_glu, conv_w, conv_b, ln_g, ln_b, glu_col0):
    b, s, _ = proj3.shape
    kw, ch = conv_w.shape
    assert kw - 1 <= CONV_HALO and glu_col0 % ch == 0
    ts = _pick(256, s)
    cb0 = glu_col0 // ch
    hb = ts // CONV_HALO
    kern = functools.partial(_conv_kernel, ts=ts, kw=kw, ch=ch)
    halo_idx = lambda off: (lambda bi, i: (bi, jnp.maximum(i * hb - 1, 0), off))
    vec = lambda n: pl.BlockSpec((1, n), lambda bi, i: (0, 0))
    return pl.pallas_call(
        kern,
        grid=(b, s // ts),
        in_specs=[pl.BlockSpec((1, ts, ch), lambda bi, i: (bi, i, cb0)),
                  pl.BlockSpec((1, ts, ch), lambda bi, i: (bi, i, cb0 + 1)),
                  pl.BlockSpec((1, CONV_HALO, ch), halo_idx(cb0)),
                  pl.BlockSpec((1, CONV_HALO, ch), halo_idx(cb0 + 1)),
                  vec(2 * ch),
                  pl.BlockSpec((kw, ch), lambda bi, i: (0, 0)),
                  vec(ch), vec(ch), vec(ch)],
        out_specs=pl.BlockSpec((1, ts, ch), lambda bi, i: (bi, i, 0)),
        out_shape=jax.ShapeDtypeStruct((b, s, ch), BF16),
        scratch_shapes=[pltpu.VMEM((SUBLANES, CONV_HALO + ts, ch), F32),
                        pltpu.VMEM((ts, ch), F32)],
        compiler_params=_cparams(("parallel", "arbitrary")),
        name="conformer_conv",
    )(proj3, proj3, proj3, proj3, b_glu, conv_w, conv_b, ln_g, ln_b)


def _outproj_kernel(a_ref, c_ref, wa_ref, wc_ref, x_ref, g_ref, b_ref, o_ref, ob_ref):
    mix = (jnp.dot(a_ref[...], wa_ref[...], preferred_element_type=F32)
           + jnp.dot(c_ref[...], wc_ref[...], preferred_element_type=F32))
    y = _ln_rows(DEEPNORM_ALPHA * x_ref[...] + mix, g_ref[...], b_ref[...])
    o_ref[...] = y
    ob_ref[...] = y.astype(BF16)


def _out_proj_ln(attn2, conv2, w_out, x2, ln_g, ln_b):
    m, d = x2.shape
    da = attn2.shape[1]
    dc = conv2.shape[1]
    tm = _pick(512, m)
    vec = pl.BlockSpec((1, d), lambda i: (0, 0))
    return pl.pallas_call(
        _outproj_kernel,
        grid=(m // tm,),
        in_specs=[pl.BlockSpec((tm, da), lambda i: (i, 0)),
                  pl.BlockSpec((tm, dc), lambda i: (i, 0)),
                  pl.BlockSpec((da, d), lambda i: (0, 0)),
                  pl.BlockSpec((dc, d), lambda i: (0, 0)),
                  pl.BlockSpec((tm, d), lambda i: (i, 0)),
                  vec, vec],
        out_specs=[pl.BlockSpec((tm, d), lambda i: (i, 0)),
                   pl.BlockSpec((tm, d), lambda i: (i, 0))],
        out_shape=[jax.ShapeDtypeStruct((m, d), F32), jax.ShapeDtypeStruct((m, d), BF16)],
        compiler_params=_cparams(("parallel",)),
        name="out_proj_ln1",
    )(attn2, conv2, w_out[:da], w_out[da:], x2, ln_g, ln_b)


SELECT_UNROLL = 2


def _top_desc(s3, k):
    tops = []
    rank = jnp.full(s3.shape, float(k), F32)
    work = s3
    for a in range(k):
        m = _allmax8(jnp.max(work, axis=0))
        hit = work == m[None]
        rank = jnp.where(hit, float(a), rank)
        work = jnp.where(hit, -jnp.inf, work)
        tops.append(m)
    return tops, rank


def _kth_largest(c3, k):
    work = c3
    m = None
    for _ in range(k):
        m = _allmax8(jnp.max(work, axis=0))
        work = jnp.where(work == m[None], -jnp.inf, work)
    return m


def _rows_of(tops, lo):
    sub = lax.broadcasted_iota(jnp.int32, tops[lo].shape, 0)
    out = tops[lo]
    for r in range(1, SUBLANES):
        out = jnp.where(sub == r, tops[lo + r], out)
    return out


def _peer_select_kernel(x_ref, wq_ref, keys_ref, n_ref, ea_ref, rb_ref, eb_ref, s_ref,
                        *, n_heads, nk, tt):
    k = PEER_TOPK
    q = jnp.dot(x_ref[...], wq_ref[...], preferred_element_type=F32).astype(BF16)
    hd = keys_ref.shape[2]
    for hc in range(2 * n_heads):
        s_ref[hc] = lax.dot_general(keys_ref[hc], q[:, hc * hd:(hc + 1) * hd],
                                    (((1,), (1,)), ((), ())), preferred_element_type=F32)

    def group(h, c0):
        cols = pl.ds(c0, LANES)
        g3 = (nk // SUBLANES, SUBLANES, LANES)
        sa = s_ref[2 * h, :, cols].reshape(g3)
        sb = s_ref[2 * h + 1, :, cols].reshape(g3)
        t1, ra = _top_desc(sa, k)
        t2, rb = _top_desc(sb, k)
        s2_lo = _rows_of(t2, 0)
        cand = jnp.stack([t1[0] + s2_lo, t1[0] + _rows_of(t2, SUBLANES)]
                         + [t1[a] + s2_lo for a in range(1, SUBLANES)]
                         + [_rows_of(t1, SUBLANES) + t2[0]])
        tau = _kth_largest(cand, k)
        sel = cand >= tau[None]
        z = _allsum8(jnp.sum(jnp.where(sel, jnp.exp(cand - (t1[0] + t2[0])[None]), 0.0), axis=0))
        picked = jnp.where(sel, 1.0, 0.0)
        n_i = jnp.where((ra < float(k)) & (sa + t2[0][None] >= tau[None]), 1.0, 0.0)
        for a in range(SUBLANES):
            n_a = _allsum8(picked[0] + picked[1] if a == 0 else picked[a + 1])
            n_i = jnp.where(ra == float(a), n_a[None], n_i)
        n_ref[h, :, cols] = n_i.reshape(nk, LANES)
        ea_ref[h, :, cols] = (jnp.exp(sa - t1[0][None]) / z[None]).reshape(nk, LANES)
        rb_ref[h, :, cols] = rb.reshape(nk, LANES).astype(BF16)
        eb_ref[h, :, cols] = jnp.exp(sb - t2[0][None]).reshape(nk, LANES).astype(BF16)

    per_head = tt // (LANES * SELECT_UNROLL)

    def body(it, carry):
        h = it // per_head
        base = (it % per_head) * (LANES * SELECT_UNROLL)
        for u in range(SELECT_UNROLL):
            group(h, pl.multiple_of(base + u * LANES, LANES))
        return carry

    lax.fori_loop(0, n_heads * per_head, body, 0)


def _peer_select(x1b, wq, keys2):
    m, d = x1b.shape
    hc, nk, hd = keys2.shape
    n_heads = hc // 2
    assert PEER_TOPK == 2 * SUBLANES and nk % SUBLANES == 0
    tt = _pick(512, m)
    kern = functools.partial(_peer_select_kernel, n_heads=n_heads, nk=nk, tt=tt)
    out_spec = pl.BlockSpec((n_heads, nk, tt), lambda i: (0, 0, i))
    shp = lambda dt: jax.ShapeDtypeStruct((n_heads, nk, m), dt)
    return pl.pallas_call(
        kern,
        grid=(m // tt,),
        in_specs=[pl.BlockSpec((tt, d), lambda i: (i, 0)),
                  pl.BlockSpec(wq.shape, lambda i: (0, 0)),
                  pl.BlockSpec(keys2.shape, lambda i: (0, 0, 0))],
        out_specs=[out_spec, out_spec, out_spec, out_spec],
        out_shape=[shp(F32), shp(F32), shp(BF16), shp(BF16)],
        scratch_shapes=[pltpu.VMEM((hc, nk, tt), F32)],
        compiler_params=_cparams(("parallel",)),
        name="peer_select",
    )(x1b, wq, keys2)


PEER_STEP_BLOCKS = 8
PEER_SLAB_BLOCKS = 2
PEER_OUT_PARTS = 2


def _peer_dense_kernel(xb_ref, u_ref, vt_ref, n_ref, ea_ref, rb_ref, eb_ref, x_ref, g_ref, b_ref,
                       o_ref, acc_ref, z_ref, w_ref, *, n_heads, ib, nk, parts):
    c = pl.program_id(1)

    @pl.when(c == 0)
    def _():
        acc_ref[...] = jnp.zeros(acc_ref.shape, F32)

    slab = PEER_SLAB_BLOCKS * nk
    for r0 in range(0, ib * nk, slab):
        z_ref[r0:r0 + slab, :] = lax.dot_general(
            u_ref[r0:r0 + slab, :], xb_ref[...], (((1,), (1,)), ((), ())),
            preferred_element_type=F32)
    g0 = pl.multiple_of(c * ib, ib)
    n_rows = [n_ref[h, pl.ds(g0, ib), :] for h in range(n_heads)]
    ea_rows = [ea_ref[h, pl.ds(g0, ib), :] for h in range(n_heads)]
    piece = (ib // parts) * nk
    pv = None
    for i in range(ib):
        r0 = i * nk
        z = z_ref[r0:r0 + nk, :]
        act = 0.5 * z * (1.0 + lax.erf(z * (2.0 ** -0.5)))
        w = jnp.zeros(z.shape, BF16)
        for h in range(n_heads):
            eb = eb_ref[h]
            hit = rb_ref[h] < n_rows[h][i:i + 1, :].astype(BF16)
            w = w + jnp.where(hit, eb, jnp.zeros_like(eb)) * ea_rows[h][i:i + 1, :].astype(BF16)
        w_ref[r0:r0 + nk, :] = act.astype(BF16) * w
        if (i + 1) * nk % piece == 0:
            h0 = (i + 1) * nk - piece
            part = jnp.dot(vt_ref[0, :, h0:h0 + piece], w_ref[h0:h0 + piece, :],
                           preferred_element_type=F32)
            pv = part if pv is None else pv + part
    acc_ref[...] += pv

    @pl.when(c == pl.num_programs(1) - 1)
    def _():
        y = DEEPNORM_ALPHA * x_ref[...] + acc_ref[...].T
        o_ref[...] = _ln_rows(y, g_ref[...], b_ref[...])


def _peer_dense(x1b, x1, u, v, sel, ln_g, ln_b, nk):
    m, d = x1.shape
    ne = u.shape[0]
    n_i, ea, rb, eb = sel
    n_heads = n_i.shape[0]
    tt = _pick(512, m)
    ib = PEER_STEP_BLOCKS
    ec = ib * nk
    vt = v.reshape(ne // ec, ec, d).transpose(0, 2, 1)
    kern = functools.partial(_peer_dense_kernel, n_heads=n_heads, ib=ib, nk=nk,
                             parts=PEER_OUT_PARTS)
    once = pl.Buffered(1)
    sel_spec = pl.BlockSpec((n_heads, nk, tt), lambda t, c: (0, 0, t), pipeline_mode=once)
    vec = pl.BlockSpec((1, d), lambda t, c: (0, 0))
    return pl.pallas_call(
        kern,
        grid=(m // tt, ne // ec),
        in_specs=[pl.BlockSpec((tt, d), lambda t, c: (t, 0)),
                  pl.BlockSpec((ec, d), lambda t, c: (c, 0)),
                  pl.BlockSpec((1, d, ec), lambda t, c: (c, 0, 0)),
                  sel_spec, sel_spec, sel_spec, sel_spec,
                  pl.BlockSpec((tt, d), lambda t, c: (t, 0), pipeline_mode=once),
                  vec, vec],
        out_specs=pl.BlockSpec((tt, d), lambda t, c: (t, 0)),
        out_shape=jax.ShapeDtypeStruct((m, d), F32),
        scratch_shapes=[pltpu.VMEM((d, tt), F32), pltpu.VMEM((ec, tt), F32),
                        pltpu.VMEM((ec, tt), BF16)],
        compiler_params=_cparams(("parallel", "arbitrary")),
        name="peer_experts_ln2",
    )(x1b, u, vt, n_i, ea, rb, eb, x1, ln_g, ln_b)


def kernel(x, w_in, b_glu, conv_w, conv_b, conv_ln_g, conv_ln_b, lambda_q1, lambda_k1, lambda_q2,
           lambda_k2, subln_g, rel_bias, w_out, ln1_g, ln1_b, peer_wq, peer_keys, peer_u, peer_v,
           ln2_g, ln2_b):
    b, s, d = x.shape
    n_layers = w_in.shape[0]
    n_heads = rel_bias.shape[1]
    qk_cols = n_heads * 2 * ATTN_HEAD_DIM
    v_cols = n_heads * ATTN_VALUE_DIM
    t_attn = _pick(512, s)
    bias = _bias_tiles(rel_bias, t_attn)
    row = lambda v: v.reshape(1, -1).astype(F32)

    x2 = x.reshape(b * s, d)
    for layer in range(n_layers):
        lambda_init = 0.8 - 0.6 * math.exp(-0.3 * layer)
        w = w_in[layer].astype(BF16)
        w_q, w_k, w_v, w_glu = (w[:, :qk_cols], w[:, qk_cols:2 * qk_cols],
                                w[:, 2 * qk_cols:2 * qk_cols + v_cols], w[:, 2 * qk_cols + v_cols:])
        kg3 = _in_proj(x2, jnp.concatenate([w_k, w_glu], axis=1), False).reshape(b, s, -1)
        qvt = _in_proj(x2, jnp.concatenate([w_q, w_v], axis=1).T, True)
        lam_params = jnp.stack([lambda_q1[layer], lambda_k1[layer],
                                lambda_q2[layer], lambda_k2[layer]]).astype(F32)
        attn = _attention(qvt, kg3, lam_params, subln_g[layer].reshape(-1, 1).astype(F32), bias,
                          n_heads, lambda_init)
        conv = _conformer_conv(kg3, row(b_glu[layer]), conv_w[layer].astype(F32),
                               row(conv_b[layer]), row(conv_ln_g[layer]), row(conv_ln_b[layer]),
                               qk_cols)
        x1, x1b = _out_proj_ln(attn.reshape(b * s, -1), conv.reshape(b * s, -1),
                               w_out[layer].astype(BF16), x2, row(ln1_g[layer]), row(ln1_b[layer]))
        keys = peer_keys[layer]
        ph, _, nk, hd = keys.shape
        sel = _peer_select(x1b, peer_wq[layer].astype(BF16),
                           keys.reshape(2 * ph, nk, hd).astype(BF16))
        x2 = _peer_dense(x1b, x1, peer_u[layer].astype(BF16), peer_v[layer].astype(BF16), sel,
                         row(ln2_g[layer]), row(ln2_b[layer]), nk)
    return x2.reshape(b, s, d)
```

```python
import functools
import math

import numpy as np

import jax
import jax.numpy as jnp
from jax import lax
from jax.experimental import pallas as pl
from jax.experimental.pallas import tpu as pltpu

F32 = jnp.float32
BF16 = jnp.bfloat16

ATTN_HEAD_DIM = 64
ATTN_VALUE_DIM = 2 * ATTN_HEAD_DIM
REL_MAX_DISTANCE = 128
PEER_TOPK = 16
LN_EPS = 1e-5
DEPTH = 1
DEEPNORM_ALPHA = (2.0 * DEPTH) ** 0.25

LANES = 128
SUBLANES = 8
VMEM_LIMIT_BYTES = 56 * 1024 * 1024


def _pick(limit, total, quantum=128):
    if total <= limit:
        return total
    t = (limit // quantum) * quantum
    while t > quantum and total % t:
        t -= quantum
    assert total % t == 0, (limit, total)
    return t


def _cparams(sem):
    return pltpu.CompilerParams(dimension_semantics=sem, vmem_limit_bytes=VMEM_LIMIT_BYTES)


def _allmax8(x):
    for sh in (4, 2, 1):
        x = jnp.maximum(x, pltpu.roll(x, sh, 0))
    return x


def _allsum8(x):
    for sh in (4, 2, 1):
        x = x + pltpu.roll(x, sh, 0)
    return x


def _ln_rows(y, g, b):
    mu = jnp.mean(y, axis=-1, keepdims=True)
    d = y - mu
    var = jnp.mean(d * d, axis=-1, keepdims=True)
    return d * lax.rsqrt(var + LN_EPS) * g + b


def _inproj_kernel(x_ref, w_ref, o_ref, xb_ref, *, transposed):
    @pl.when(pl.program_id(1) == 0)
    def _():
        xb_ref[...] = x_ref[...].astype(BF16)

    if transposed:
        o = lax.dot_general(w_ref[...], xb_ref[...], (((1,), (1,)), ((), ())),
                            preferred_element_type=F32)
    else:
        o = jnp.dot(xb_ref[...], w_ref[...], preferred_element_type=F32)
    o_ref[...] = o.astype(o_ref.dtype)


def _in_proj(x2, w, transposed):
    m, d = x2.shape
    n = w.shape[0] if transposed else w.shape[1]
    tm = _pick(1024, m)
    tn = _pick(1024, n)
    if transposed:
        w_spec = pl.BlockSpec((tn, d), lambda i, j: (j, 0))
        o_spec = pl.BlockSpec((tn, tm), lambda i, j: (j, i))
        o_shape = jax.ShapeDtypeStruct((n, m), BF16)
    else:
        w_spec = pl.BlockSpec((d, tn), lambda i, j: (0, j))
        o_spec = pl.BlockSpec((tm, tn), lambda i, j: (i, j))
        o_shape = jax.ShapeDtypeStruct((m, n), BF16)
    return pl.pallas_call(
        functools.partial(_inproj_kernel, transposed=transposed),
        grid=(m // tm, n // tn),
        in_specs=[pl.BlockSpec((tm, d), lambda i, j: (i, 0)), w_spec],
        out_specs=o_spec,
        out_shape=o_shape,
        scratch_shapes=[pltpu.VMEM((tm, d), BF16)],
        compiler_params=_cparams(("parallel", "arbitrary")),
        name="in_proj_t" if transposed else "in_proj",
    )(x2, w)


LOG2E = math.log2(math.e)
ONES_ROWS = 16


def _t5_bucket_np(dist, num_buckets):
    max_exact = num_buckets // 2
    n_f = np.maximum(dist, 1).astype(np.float32)
    large = max_exact + (np.log(n_f / np.float32(max_exact))
                         / np.float32(math.log(REL_MAX_DISTANCE / max_exact))
                         * np.float32(num_buckets - max_exact)).astype(np.int32)
    large = np.minimum(large, num_buckets - 1)
    return np.where(dist < max_exact, dist, large).astype(np.int32)


def _bias_kernel(rb_ref, bucket_ref, o_ref, *, nb):
    h = pl.program_id(0)
    bucket = bucket_ref[...]
    tile = jnp.zeros(bucket.shape, F32)
    for bkt in range(nb):
        tile = jnp.where(bucket == bkt, rb_ref[bkt, h], tile)
    tile = (tile - rb_ref[nb - 1, h]) * LOG2E
    o_ref[0] = jnp.where(bucket < 0, -jnp.inf, tile)


def _bias_tiles(rel_bias, t):
    nb, n_heads = rel_bias.shape
    assert t + 1 >= REL_MAX_DISTANCE
    key = np.arange(t, dtype=np.int32)[:, None]
    qry = np.arange(t, dtype=np.int32)[None, :]
    d_diag = qry - key
    diag = np.where(d_diag >= 0, _t5_bucket_np(np.maximum(d_diag, 0), nb), -1)
    prev = _t5_bucket_np(d_diag + t, nb)
    buckets = jnp.asarray(np.stack([prev, diag]).astype(np.int32))
    return pl.pallas_call(
        functools.partial(_bias_kernel, nb=nb),
        grid=(n_heads,),
        in_specs=[pl.BlockSpec(memory_space=pltpu.SMEM),
                  pl.BlockSpec((2, t, t), lambda h: (0, 0, 0))],
        out_specs=pl.BlockSpec((1, 2, t, t), lambda h: (h, 0, 0, 0)),
        out_shape=jax.ShapeDtypeStruct((n_heads, 2, t, t), F32),
        compiler_params=_cparams(("arbitrary",)),
        name="rel_bias_tiles",
    )(rel_bias.astype(F32), buckets)


def _attn_kernel(lam_ref, g_ref, qt_ref, k_ref, vt_ref, bias_ref, o_ref, vext_ref, sa_ref, sb_ref,
                    pa_ref, pb_ref, acc_ref, *, t, lambda_init):
    qi = pl.program_id(2)
    dv = ATTN_VALUE_DIM

    @pl.when(qi == 0)
    def _():
        vext_ref[0:dv, :] = vt_ref[...]
        vext_ref[dv:, :] = jnp.ones((ONES_ROWS, vext_ref.shape[1]), BF16)

    qf = qt_ref[...].astype(F32) * (ATTN_HEAD_DIM ** -0.5 * LOG2E)
    row = lax.broadcasted_iota(jnp.int32, qf.shape, 0)
    comps = (jnp.where(row < ATTN_HEAD_DIM, qf, 0.0).astype(BF16),
             jnp.where(row >= ATTN_HEAD_DIM, qf, 0.0).astype(BF16))
    s_bufs = (sa_ref, sb_ref)
    p_bufs = (pa_ref, pb_ref)

    def qk(j, buf):
        kj = k_ref[0, pl.ds(pl.multiple_of(j * t, t), t), :]
        for c in range(2):
            s_bufs[buf][c] = jnp.dot(kj, comps[c], preferred_element_type=F32)

    def pv(j, buf, alphas):
        vj = vext_ref[:, pl.ds(pl.multiple_of(j * t, t), t)]
        for c in range(2):
            acc3 = acc_ref[c].reshape(-1, SUBLANES, t) * alphas[c][None]
            acc_ref[c] = (acc3.reshape(acc_ref.shape[1:])
                          + jnp.dot(vj, p_bufs[buf][c], preferred_element_type=F32))

    def step(j, buf, bias_idx, state, prefetch):
        ms, alphas = state
        if prefetch:
            qk(j + 1, 1 - buf)
        pv(jnp.maximum(j - 1, 0), 1 - buf, alphas)
        s_ref, p_ref = s_bufs[buf], p_bufs[buf]

        def rows(c, r0, n):
            s = s_ref[c, r0:r0 + n, :]
            return s if bias_idx is None else s + bias_ref[0, bias_idx, r0:r0 + n, :]

        out = []
        for c in range(2):
            m8 = rows(c, 0, SUBLANES)
            for r0 in range(SUBLANES, t, SUBLANES):
                m8 = jnp.maximum(m8, rows(c, r0, SUBLANES))
            m_new = jnp.maximum(ms[c], _allmax8(m8))
            m16 = jnp.concatenate([m_new, m_new], axis=0)
            for r0 in range(0, t, 2 * SUBLANES):
                p_ref[c, r0:r0 + 2 * SUBLANES, :] = jnp.exp2(rows(c, r0, 2 * SUBLANES) - m16).astype(BF16)
            out.append((m_new, jnp.exp2(ms[c] - m_new)))
        return tuple(m for m, _ in out), tuple(a for _, a in out)

    acc_ref[...] = jnp.zeros(acc_ref.shape, F32)
    pb_ref[...] = jnp.zeros(pb_ref.shape, BF16)
    m0 = jnp.full((SUBLANES, t), -jnp.inf, F32)
    one = jnp.ones((SUBLANES, t), F32)
    qk(0, 0)
    n_far = jnp.maximum(qi - 1, 0)

    def pair(i, state):
        state = step(2 * i, 0, None, state, True)
        return step(2 * i + 1, 1, None, state, True)

    state = lax.fori_loop(0, n_far // 2, pair, ((m0, m0), (one, one)))

    def tail_first(state):
        _, alphas = step(qi, 0, 1, state, False)
        pv(qi, 0, alphas)
        return 0

    def tail_even(state):
        state = step(qi - 1, 0, 0, state, True)
        _, alphas = step(qi, 1, 1, state, False)
        pv(qi, 1, alphas)
        return 0

    def tail_odd(state):
        state = step(qi - 2, 0, None, state, True)
        state = step(qi - 1, 1, 0, state, True)
        _, alphas = step(qi, 0, 1, state, False)
        pv(qi, 0, alphas)
        return 0

    branch = jnp.where(qi == 0, 0, 1 + n_far % 2)
    lax.switch(branch, [tail_first, tail_even, tail_odd], state)

    lam_p = lam_ref[...]
    lam = (jnp.exp(jnp.sum(lam_p[0:1] * lam_p[1:2], axis=1, keepdims=True))
           - jnp.exp(jnp.sum(lam_p[2:3] * lam_p[3:4], axis=1, keepdims=True)) + lambda_init)
    a1 = acc_ref[0]
    a2 = acc_ref[1]
    o = a1[:dv] / a1[dv:dv + 1] - lam * (a2[:dv] / a2[dv:dv + 1])
    y = o * lax.rsqrt(jnp.mean(o * o, axis=0, keepdims=True) + LN_EPS)
    y = y * (g_ref[...] * (1.0 - lambda_init))
    o_ref[0] = y.T.astype(o_ref.dtype)


def _attention(qvt, k3, lam_params, subln_g_col, bias, n_heads, lambda_init):
    b, s, _ = k3.shape
    t = bias.shape[-1]
    dv = ATTN_VALUE_DIM
    nq = s // t
    kern = functools.partial(_attn_kernel, t=t, lambda_init=lambda_init)
    return pl.pallas_call(
        kern,
        grid=(b, n_heads, nq),
        in_specs=[pl.BlockSpec(lam_params.shape, lambda bi, h, qi: (0, 0)),
                  pl.BlockSpec((dv, 1), lambda bi, h, qi: (0, 0)),
                  pl.BlockSpec((dv, t), lambda bi, h, qi: (h, bi * nq + qi)),
                  pl.BlockSpec((1, s, dv), lambda bi, h, qi: (bi, 0, h)),
                  pl.BlockSpec((dv, s), lambda bi, h, qi: (n_heads + h, bi)),
                  pl.BlockSpec((1, 2, t, t), lambda bi, h, qi: (h, 0, 0, 0))],
        out_specs=pl.BlockSpec((1, t, dv), lambda bi, h, qi: (bi, qi, h)),
        out_shape=jax.ShapeDtypeStruct((b, s, n_heads * dv), BF16),
        scratch_shapes=[pltpu.VMEM((dv + ONES_ROWS, s), BF16),
                        pltpu.VMEM((2, t, t), F32), pltpu.VMEM((2, t, t), F32),
                        pltpu.VMEM((2, t, t), BF16), pltpu.VMEM((2, t, t), BF16),
                        pltpu.VMEM((2, dv + ONES_ROWS, t), F32)],
        compiler_params=_cparams(("parallel", "parallel", "arbitrary")),
        name="diff_attention",
    )(lam_params, subln_g_col, qvt, k3, qvt, bias)


CONV_HALO = 32
CONV_ROW_CHUNK = 64


def _conv_kernel(a_ref, g_ref, ah_ref, gh_ref, bglu_ref, w_ref, cb_ref, lng_ref, lnb_ref, o_ref,
                 h_ref, c_ref, *, ts, kw, ch):
    i = pl.program_id(1)
    ba = bglu_ref[:, :ch]
    bg = bglu_ref[:, ch:]

    def glu(a, g):
        return (a.astype(F32) + ba) * jax.nn.sigmoid(g.astype(F32) + bg)

    halo = glu(ah_ref[0], gh_ref[0])
    h_ref[0, 0:CONV_HALO, :] = jnp.where(i > 0, halo, jnp.zeros_like(halo))
    h_ref[0, CONV_HALO:, :] = glu(a_ref[0], g_ref[0])
    lead = CONV_HALO - (kw - 1)
    span = ts + ((lead + kw - 1) // SUBLANES) * SUBLANES
    for b in range(1, SUBLANES):
        n = min(span, CONV_HALO + ts - b)
        h_ref[b, 0:n, :] = h_ref[0, b:b + n, :]

    rc = min(CONV_ROW_CHUNK, ts)
    for c in range(ch // LANES):
        cs = slice(c * LANES, (c + 1) * LANES)
        wc = w_ref[:, cs]
        cb = cb_ref[:, cs]

        accs = [jnp.zeros((rc // SUBLANES, SUBLANES, LANES), F32) for _ in range(0, ts, rc)]
        for k in range(kw):
            a, b = divmod(lead + k, SUBLANES)
            w8 = jnp.broadcast_to(wc[k:k + 1, :], (SUBLANES, LANES))[None]
            for ri, r0 in enumerate(range(0, ts, rc)):
                off = r0 + a * SUBLANES
                accs[ri] = accs[ri] + h_ref[b, off:off + rc, cs].reshape(accs[ri].shape) * w8
        for ri, r0 in enumerate(range(0, ts, rc)):
            c_ref[r0:r0 + rc, cs] = accs[ri].reshape(rc, LANES) + cb

    y = _ln_rows(c_ref[...], lng_ref[...], lnb_ref[...])
    o_ref[0] = (y * jax.nn.sigmoid(y)).astype(o_ref.dtype)


def _conformer_conv(proj3, b_glu, conv_w, conv_b, ln_g, ln_b, glu_col0):
    b, s, _ = proj3.shape
    kw, ch = conv_w.shape
    assert kw - 1 <= CONV_HALO and glu_col0 % ch == 0
    ts = _pick(256, s)
    cb0 = glu_col0 // ch
    hb = ts // CONV_HALO
    kern = functools.partial(_conv_kernel, ts=ts, kw=kw, ch=ch)
    halo_idx = lambda off: (lambda bi, i: (bi, jnp.maximum(i * hb - 1, 0), off))
    vec = lambda n: pl.BlockSpec((1, n), lambda bi, i: (0, 0))
    return pl.pallas_call(
        kern,
        grid=(b, s // ts),
        in_specs=[pl.BlockSpec((1, ts, ch), lambda bi, i: (bi, i, cb0)),
                  pl.BlockSpec((1, ts, ch), lambda bi, i: (bi, i, cb0 + 1)),
                  pl.BlockSpec((1, CONV_HALO, ch), halo_idx(cb0)),
                  pl.BlockSpec((1, CONV_HALO, ch), halo_idx(cb0 + 1)),
                  vec(2 * ch),
                  pl.BlockSpec((kw, ch), lambda bi, i: (0, 0)),
                  vec(ch), vec(ch), vec(ch)],
        out_specs=pl.BlockSpec((1, ts, ch), lambda bi, i: (bi, i, 0)),
        out_shape=jax.ShapeDtypeStruct((b, s, ch), BF16),
        scratch_shapes=[pltpu.VMEM((SUBLANES, CONV_HALO + ts, ch), F32),
                        pltpu.VMEM((ts, ch), F32)],
        compiler_params=_cparams(("parallel", "arbitrary")),
        name="conformer_conv",
    )(proj3, proj3, proj3, proj3, b_glu, conv_w, conv_b, ln_g, ln_b)


def _outproj_kernel(a_ref, c_ref, wa_ref, wc_ref, x_ref, g_ref, b_ref, o_ref, ob_ref, obt_ref):
    mix = (jnp.dot(a_ref[...], wa_ref[...], preferred_element_type=F32)
           + jnp.dot(c_ref[...], wc_ref[...], preferred_element_type=F32))
    y = _ln_rows(DEEPNORM_ALPHA * x_ref[...] + mix, g_ref[...], b_ref[...])
    o_ref[...] = y
    ob_ref[...] = y.astype(BF16)
    obt_ref[...] = y.T.astype(BF16)


def _out_proj_ln(attn2, conv2, w_out, x2, ln_g, ln_b):
    m, d = x2.shape
    da = attn2.shape[1]
    dc = conv2.shape[1]
    tm = _pick(512, m)
    vec = pl.BlockSpec((1, d), lambda i: (0, 0))
    return pl.pallas_call(
        _outproj_kernel,
        grid=(m // tm,),
        in_specs=[pl.BlockSpec((tm, da), lambda i: (i, 0)),
                  pl.BlockSpec((tm, dc), lambda i: (i, 0)),
                  pl.BlockSpec((da, d), lambda i: (0, 0)),
                  pl.BlockSpec((dc, d), lambda i: (0, 0)),
                  pl.BlockSpec((tm, d), lambda i: (i, 0)),
                  vec, vec],
        out_specs=[pl.BlockSpec((tm, d), lambda i: (i, 0)),
                   pl.BlockSpec((tm, d), lambda i: (i, 0)),
                   pl.BlockSpec((d, tm), lambda i: (0, i))],
        out_shape=[jax.ShapeDtypeStruct((m, d), F32), jax.ShapeDtypeStruct((m, d), BF16),
                   jax.ShapeDtypeStruct((d, m), BF16)],
        compiler_params=_cparams(("parallel",)),
        name="out_proj_ln1",
    )(attn2, conv2, w_out[:da], w_out[da:], x2, ln_g, ln_b)


SELECT_UNROLL = 2


def _top_desc(s3, k):
    tops = []
    rank = jnp.full(s3.shape, float(k), F32)
    work = s3
    for a in range(k):
        m = _allmax8(jnp.max(work, axis=0))
        hit = work == m[None]
        rank = jnp.where(hit, float(a), rank)
        work = jnp.where(hit, -jnp.inf, work)
        tops.append(m)
    return tops, rank


def _kth_largest(c3, k):
    work = c3
    m = None
    for _ in range(k):
        m = _allmax8(jnp.max(work, axis=0))
        work = jnp.where(work == m[None], -jnp.inf, work)
    return m


def _rows_of(tops, lo):
    sub = lax.broadcasted_iota(jnp.int32, tops[lo].shape, 0)
    out = tops[lo]
    for r in range(1, SUBLANES):
        out = jnp.where(sub == r, tops[lo + r], out)
    return out


def _peer_select_kernel(x_ref, wq_ref, keys_ref, n_ref, ea_ref, rb_ref, eb_ref, s_ref,
                        *, n_heads, nk, tt):
    k = PEER_TOPK
    q = jnp.dot(x_ref[...], wq_ref[...], preferred_element_type=F32).astype(BF16)
    hd = keys_ref.shape[2]
    for hc in range(2 * n_heads):
        s_ref[hc] = lax.dot_general(keys_ref[hc], q[:, hc * hd:(hc + 1) * hd],
                                    (((1,), (1,)), ((), ())), preferred_element_type=F32)

    def group(h, c0):
        cols = pl.ds(c0, LANES)
        g3 = (nk // SUBLANES, SUBLANES, LANES)
        sa = s_ref[2 * h, :, cols].reshape(g3)
        sb = s_ref[2 * h + 1, :, cols].reshape(g3)
        t1, ra = _top_desc(sa, k)
        t2, rb = _top_desc(sb, k)
        s2_lo = _rows_of(t2, 0)
        cand = jnp.stack([t1[0] + s2_lo, t1[0] + _rows_of(t2, SUBLANES)]
                         + [t1[a] + s2_lo for a in range(1, SUBLANES)]
                         + [_rows_of(t1, SUBLANES) + t2[0]])
        tau = _kth_largest(cand, k)
        sel = cand >= tau[None]
        z = _allsum8(jnp.sum(jnp.where(sel, jnp.exp(cand - (t1[0] + t2[0])[None]), 0.0), axis=0))
        picked = jnp.where(sel, 1.0, 0.0)
        n_i = jnp.where((ra < float(k)) & (sa + t2[0][None] >= tau[None]), 1.0, 0.0)
        for a in range(SUBLANES):
            n_a = _allsum8(picked[0] + picked[1] if a == 0 else picked[a + 1])
            n_i = jnp.where(ra == float(a), n_a[None], n_i)
        n_ref[h, :, cols] = n_i.reshape(nk, LANES)
        ea_ref[h, :, cols] = (jnp.exp(sa - t1[0][None]) / z[None]).reshape(nk, LANES)
        rb_ref[h, :, cols] = rb.reshape(nk, LANES).astype(BF16)
        eb_ref[h, :, cols] = jnp.exp(sb - t2[0][None]).reshape(nk, LANES).astype(BF16)

    per_head = tt // (LANES * SELECT_UNROLL)

    def body(it, carry):
        h = it // per_head
        base = (it % per_head) * (LANES * SELECT_UNROLL)
        for u in range(SELECT_UNROLL):
            group(h, pl.multiple_of(base + u * LANES, LANES))
        return carry

    lax.fori_loop(0, n_heads * per_head, body, 0)


def _peer_select(x1b, wq, keys2):
    m, d = x1b.shape
    hc, nk, hd = keys2.shape
    n_heads = hc // 2
    assert PEER_TOPK == 2 * SUBLANES and nk % SUBLANES == 0
    tt = _pick(512, m)
    kern = functools.partial(_peer_select_kernel, n_heads=n_heads, nk=nk, tt=tt)
    out_spec = pl.BlockSpec((n_heads, nk, tt), lambda i: (0, 0, i))
    shp = lambda dt: jax.ShapeDtypeStruct((n_heads, nk, m), dt)
    return pl.pallas_call(
        kern,
        grid=(m // tt,),
        in_specs=[pl.BlockSpec((tt, d), lambda i: (i, 0)),
                  pl.BlockSpec(wq.shape, lambda i: (0, 0)),
                  pl.BlockSpec(keys2.shape, lambda i: (0, 0, 0))],
        out_specs=[out_spec, out_spec, out_spec, out_spec],
        out_shape=[shp(F32), shp(F32), shp(BF16), shp(BF16)],
        scratch_shapes=[pltpu.VMEM((hc, nk, tt), F32)],
        compiler_params=_cparams(("parallel",)),
        name="peer_select",
    )(x1b, wq, keys2)


PEER_STEP_BLOCKS = 8
PEER_SLAB_BLOCKS = 2
PEER_OUT_PARTS = 2


def _peer_dense_kernel(xt_ref, u_ref, vt_ref, n_ref, ea_ref, rb_ref, eb_ref, x_ref, g_ref, b_ref,
                       o_ref, acc_ref, *zw_refs, n_heads, ib, nk, parts):
    c = pl.program_id(1)
    n_slabs = ib // PEER_SLAB_BLOCKS
    z_refs, w_refs = zw_refs[:n_slabs], zw_refs[n_slabs:]

    @pl.when(c == 0)
    def _():
        acc_ref[...] = jnp.zeros(acc_ref.shape, F32)

    slab = PEER_SLAB_BLOCKS * nk
    piece = (ib // parts) * nk
    g0 = pl.multiple_of(c * ib, ib)
    n_rows = [n_ref[h, pl.ds(g0, ib), :] for h in range(n_heads)]
    ea_rows = [ea_ref[h, pl.ds(g0, ib), :] for h in range(n_heads)]

    def pre_activation(si):
        z_refs[si][...] = jnp.dot(u_ref[si * slab:(si + 1) * slab, :], xt_ref[...],
                                  preferred_element_type=F32)

    def gate(i):
        si, zi = divmod(i * nk, slab)
        z = z_refs[si][zi:zi + nk, :]
        act = 0.5 * z * (1.0 + lax.erf(z * (2.0 ** -0.5)))
        w = jnp.zeros(z.shape, BF16)
        for h in range(n_heads):
            eb = eb_ref[h]
            hit = rb_ref[h] < n_rows[h][i:i + 1, :].astype(BF16)
            w = w + jnp.where(hit, eb, jnp.zeros_like(eb)) * ea_rows[h][i:i + 1, :].astype(BF16)
        pi, wi = divmod(i * nk, piece)
        w_refs[pi][wi:wi + nk, :] = act.astype(BF16) * w

    pre_activation(0)
    if n_slabs > 1:
        pre_activation(1)
    pv = None
    for i in range(ib):
        si = (i * nk) // slab
        if (i * nk) % slab == 0 and si + 2 < n_slabs:
            pre_activation(si + 2)
        gate(i)
        if (i + 1) * nk % piece == 0:
            pi = (i * nk) // piece
            part = jnp.dot(vt_ref[0, :, pi * piece:(pi + 1) * piece], w_refs[pi][...],
                           preferred_element_type=F32)
            pv = part if pv is None else pv + part
    acc_ref[...] += pv

    @pl.when(c == pl.num_programs(1) - 1)
    def _():
        y = DEEPNORM_ALPHA * x_ref[...] + acc_ref[...].T
        o_ref[...] = _ln_rows(y, g_ref[...], b_ref[...])


def _peer_dense(x1bt, x1, u, v, sel, ln_g, ln_b, nk):
    m, d = x1.shape
    ne = u.shape[0]
    n_i, ea, rb, eb = sel
    n_heads = n_i.shape[0]
    tt = _pick(512, m)
    ib = PEER_STEP_BLOCKS
    ec = ib * nk
    vt = v.reshape(ne // ec, ec, d).transpose(0, 2, 1)
    kern = functools.partial(_peer_dense_kernel, n_heads=n_heads, ib=ib, nk=nk,
                             parts=PEER_OUT_PARTS)
    once = pl.Buffered(1)
    sel_spec = pl.BlockSpec((n_heads, nk, tt), lambda t, c: (0, 0, t), pipeline_mode=once)
    vec = pl.BlockSpec((1, d), lambda t, c: (0, 0))
    return pl.pallas_call(
        kern,
        grid=(m // tt, ne // ec),
        in_specs=[pl.BlockSpec((d, tt), lambda t, c: (0, t)),
                  pl.BlockSpec((ec, d), lambda t, c: (c, 0)),
                  pl.BlockSpec((1, d, ec), lambda t, c: (c, 0, 0)),
                  sel_spec, sel_spec, sel_spec, sel_spec,
                  pl.BlockSpec((tt, d), lambda t, c: (t, 0), pipeline_mode=once),
                  vec, vec],
        out_specs=pl.BlockSpec((tt, d), lambda t, c: (t, 0)),
        out_shape=jax.ShapeDtypeStruct((m, d), F32),
        scratch_shapes=([pltpu.VMEM((d, tt), F32)]
                        + [pltpu.VMEM((PEER_SLAB_BLOCKS * nk, tt), F32)] * (ib // PEER_SLAB_BLOCKS)
                        + [pltpu.VMEM((ec // PEER_OUT_PARTS, tt), BF16)] * PEER_OUT_PARTS),
        compiler_params=_cparams(("parallel", "arbitrary")),
        name="peer_experts_ln2",
    )(x1bt, u, vt, n_i, ea, rb, eb, x1, ln_g, ln_b)


def kernel(x, w_in, b_glu, conv_w, conv_b, conv_ln_g, conv_ln_b, lambda_q1, lambda_k1, lambda_q2,
           lambda_k2, subln_g, rel_bias, w_out, ln1_g, ln1_b, peer_wq, peer_keys, peer_u, peer_v,
           ln2_g, ln2_b):
    b, s, d = x.shape
    n_layers = w_in.shape[0]
    n_heads = rel_bias.shape[1]
    qk_cols = n_heads * 2 * ATTN_HEAD_DIM
    v_cols = n_heads * ATTN_VALUE_DIM
    t_attn = _pick(512, s)
    bias = _bias_tiles(rel_bias, t_attn)
    row = lambda v: v.reshape(1, -1).astype(F32)

    x2 = x.reshape(b * s, d)
    for layer in range(n_layers):
        lambda_init = 0.8 - 0.6 * math.exp(-0.3 * layer)
        w = w_in[layer].astype(BF16)
        w_q, w_k, w_v, w_glu = (w[:, :qk_cols], w[:, qk_cols:2 * qk_cols],
                                w[:, 2 * qk_cols:2 * qk_cols + v_cols], w[:, 2 * qk_cols + v_cols:])
        kg3 = _in_proj(x2, jnp.concatenate([w_k, w_glu], axis=1), False).reshape(b, s, -1)
        qvt = _in_proj(x2, jnp.concatenate([w_q, w_v], axis=1).T, True)
        lam_params = jnp.stack([lambda_q1[layer], lambda_k1[layer],
                                lambda_q2[layer], lambda_k2[layer]]).astype(F32)
        attn = _attention(qvt, kg3, lam_params, subln_g[layer].reshape(-1, 1).astype(F32), bias,
                          n_heads, lambda_init)
        conv = _conformer_conv(kg3, row(b_glu[layer]), conv_w[layer].astype(F32),
                               row(conv_b[layer]), row(conv_ln_g[layer]), row(conv_ln_b[layer]),
                               qk_cols)
        x1, x1b, x1bt = _out_proj_ln(attn.reshape(b * s, -1), conv.reshape(b * s, -1),
                                     w_out[layer].astype(BF16), x2, row(ln1_g[layer]),
                                     row(ln1_b[layer]))
        keys = peer_keys[layer]
        ph, _, nk, hd = keys.shape
        sel = _peer_select(x1b, peer_wq[layer].astype(BF16),
                           keys.reshape(2 * ph, nk, hd).astype(BF16))
        x2 = _peer_dense(x1bt, x1, peer_u[layer].astype(BF16), peer_v[layer].astype(BF16), sel,
                         row(ln2_g[layer]), row(ln2_b[layer]), nk)
    return x2.reshape(b, s, d)
```

```python
import functools
import math

import numpy as np

import jax
import jax.numpy as jnp
from jax import lax
from jax.experimental import pallas as pl
from jax.experimental.pallas import tpu as pltpu

F32 = jnp.float32
BF16 = jnp.bfloat16

ATTN_HEAD_DIM = 64
ATTN_VALUE_DIM = 2 * ATTN_HEAD_DIM
REL_MAX_DISTANCE = 128
PEER_TOPK = 16
LN_EPS = 1e-5
DEPTH = 1
DEEPNORM_ALPHA = (2.0 * DEPTH) ** 0.25

LANES = 128
SUBLANES = 8
VMEM_LIMIT_BYTES = 56 * 1024 * 1024


def _pick(limit, total, quantum=128):
    if total <= limit:
        return total
    t = (limit // quantum) * quantum
    while t > quantum and total % t:
        t -= quantum
    assert total % t == 0, (limit, total)
    return t


def _cparams(sem):
    return pltpu.CompilerParams(dimension_semantics=sem, vmem_limit_bytes=VMEM_LIMIT_BYTES)


def _allmax8(x):
    for sh in (4, 2, 1):
        x = jnp.maximum(x, pltpu.roll(x, sh, 0))
    return x


def _allsum8(x):
    for sh in (4, 2, 1):
        x = x + pltpu.roll(x, sh, 0)
    return x


def _ln_rows(y, g, b):
    mu = jnp.mean(y, axis=-1, keepdims=True)
    d = y - mu
    var = jnp.mean(d * d, axis=-1, keepdims=True)
    return d * lax.rsqrt(var + LN_EPS) * g + b


def _inproj_kernel(x_ref, w_ref, o_ref, xb_ref, *, transposed):
    @pl.when(pl.program_id(1) == 0)
    def _():
        xb_ref[...] = x_ref[...].astype(BF16)

    if transposed:
        o = lax.dot_general(w_ref[...], xb_ref[...], (((1,), (1,)), ((), ())),
                            preferred_element_type=F32)
    else:
        o = jnp.dot(xb_ref[...], w_ref[...], preferred_element_type=F32)
    o_ref[...] = o.astype(o_ref.dtype)


def _in_proj(x2, w, transposed):
    m, d = x2.shape
    n = w.shape[0] if transposed else w.shape[1]
    tm = _pick(1024, m)
    tn = _pick(1024, n)
    if transposed:
        w_spec = pl.BlockSpec((tn, d), lambda i, j: (j, 0))
        o_spec = pl.BlockSpec((tn, tm), lambda i, j: (j, i))
        o_shape = jax.ShapeDtypeStruct((n, m), BF16)
    else:
        w_spec = pl.BlockSpec((d, tn), lambda i, j: (0, j))
        o_spec = pl.BlockSpec((tm, tn), lambda i, j: (i, j))
        o_shape = jax.ShapeDtypeStruct((m, n), BF16)
    return pl.pallas_call(
        functools.partial(_inproj_kernel, transposed=transposed),
        grid=(m // tm, n // tn),
        in_specs=[pl.BlockSpec((tm, d), lambda i, j: (i, 0)), w_spec],
        out_specs=o_spec,
        out_shape=o_shape,
        scratch_shapes=[pltpu.VMEM((tm, d), BF16)],
        compiler_params=_cparams(("parallel", "arbitrary")),
        name="in_proj_t" if transposed else "in_proj",
    )(x2, w)


LOG2E = math.log2(math.e)
ONES_ROWS = 16


def _t5_bucket_np(dist, num_buckets):
    max_exact = num_buckets // 2
    n_f = np.maximum(dist, 1).astype(np.float32)
    large = max_exact + (np.log(n_f / np.float32(max_exact))
                         / np.float32(math.log(REL_MAX_DISTANCE / max_exact))
                         * np.float32(num_buckets - max_exact)).astype(np.int32)
    large = np.minimum(large, num_buckets - 1)
    return np.where(dist < max_exact, dist, large).astype(np.int32)


def _bias_kernel(rb_ref, bucket_ref, o_ref, *, nb):
    h = pl.program_id(0)
    bucket = bucket_ref[...]
    tile = jnp.zeros(bucket.shape, F32)
    for bkt in range(nb):
        tile = jnp.where(bucket == bkt, rb_ref[bkt, h], tile)
    tile = (tile - rb_ref[nb - 1, h]) * LOG2E
    o_ref[0] = jnp.where(bucket < 0, -jnp.inf, tile)


def _bias_tiles(rel_bias, t):
    nb, n_heads = rel_bias.shape
    assert t + 1 >= REL_MAX_DISTANCE
    key = np.arange(t, dtype=np.int32)[:, None]
    qry = np.arange(t, dtype=np.int32)[None, :]
    d_diag = qry - key
    diag = np.where(d_diag >= 0, _t5_bucket_np(np.maximum(d_diag, 0), nb), -1)
    prev = _t5_bucket_np(d_diag + t, nb)
    buckets = jnp.asarray(np.stack([prev, diag]).astype(np.int32))
    return pl.pallas_call(
        functools.partial(_bias_kernel, nb=nb),
        grid=(n_heads,),
        in_specs=[pl.BlockSpec(memory_space=pltpu.SMEM),
                  pl.BlockSpec((2, t, t), lambda h: (0, 0, 0))],
        out_specs=pl.BlockSpec((1, 2, t, t), lambda h: (h, 0, 0, 0)),
        out_shape=jax.ShapeDtypeStruct((n_heads, 2, t, t), F32),
        compiler_params=_cparams(("arbitrary",)),
        name="rel_bias_tiles",
    )(rel_bias.astype(F32), buckets)


def _attn_kernel(lam_ref, g_ref, qt_ref, k_ref, vt_ref, bias_ref, o_ref, vext_ref, sa_ref, sb_ref,
                    pa_ref, pb_ref, acc_ref, *, t, lambda_init):
    qi = pl.program_id(2)
    dv = ATTN_VALUE_DIM

    @pl.when(qi == 0)
    def _():
        vext_ref[0:dv, :] = vt_ref[...]
        vext_ref[dv:, :] = jnp.ones((ONES_ROWS, vext_ref.shape[1]), BF16)

    qf = qt_ref[...].astype(F32) * (ATTN_HEAD_DIM ** -0.5 * LOG2E)
    row = lax.broadcasted_iota(jnp.int32, qf.shape, 0)
    comps = (jnp.where(row < ATTN_HEAD_DIM, qf, 0.0).astype(BF16),
             jnp.where(row >= ATTN_HEAD_DIM, qf, 0.0).astype(BF16))
    s_bufs = (sa_ref, sb_ref)
    p_bufs = (pa_ref, pb_ref)

    def qk(j, buf):
        kj = k_ref[0, pl.ds(pl.multiple_of(j * t, t), t), :]
        for c in range(2):
            s_bufs[buf][c] = jnp.dot(kj, comps[c], preferred_element_type=F32)

    def pv(j, buf, alphas):
        vj = vext_ref[:, pl.ds(pl.multiple_of(j * t, t), t)]
        for c in range(2):
            acc3 = acc_ref[c].reshape(-1, SUBLANES, t) * alphas[c][None]
            acc_ref[c] = (acc3.reshape(acc_ref.shape[1:])
                          + jnp.dot(vj, p_bufs[buf][c], preferred_element_type=F32))

    def step(j, buf, bias_idx, state, prefetch):
        ms, alphas = state
        if prefetch:
            qk(j + 1, 1 - buf)
        pv(jnp.maximum(j - 1, 0), 1 - buf, alphas)
        s_ref, p_ref = s_bufs[buf], p_bufs[buf]

        def rows(c, r0, n):
            s = s_ref[c, r0:r0 + n, :]
            return s if bias_idx is None else s + bias_ref[0, bias_idx, r0:r0 + n, :]

        out = []
        for c in range(2):
            m8 = rows(c, 0, SUBLANES)
            for r0 in range(SUBLANES, t, SUBLANES):
                m8 = jnp.maximum(m8, rows(c, r0, SUBLANES))
            m_new = jnp.maximum(ms[c], _allmax8(m8))
            m16 = jnp.concatenate([m_new, m_new], axis=0)
            for r0 in range(0, t, 2 * SUBLANES):
                p_ref[c, r0:r0 + 2 * SUBLANES, :] = jnp.exp2(rows(c, r0, 2 * SUBLANES) - m16).astype(BF16)
            out.append((m_new, jnp.exp2(ms[c] - m_new)))
        return tuple(m for m, _ in out), tuple(a for _, a in out)

    acc_ref[...] = jnp.zeros(acc_ref.shape, F32)
    pb_ref[...] = jnp.zeros(pb_ref.shape, BF16)
    m0 = jnp.full((SUBLANES, t), -jnp.inf, F32)
    one = jnp.ones((SUBLANES, t), F32)
    qk(0, 0)
    n_far = jnp.maximum(qi - 1, 0)

    def pair(i, state):
        state = step(2 * i, 0, None, state, True)
        return step(2 * i + 1, 1, None, state, True)

    state = lax.fori_loop(0, n_far // 2, pair, ((m0, m0), (one, one)))

    def tail_first(state):
        _, alphas = step(qi, 0, 1, state, False)
        pv(qi, 0, alphas)
        return 0

    def tail_even(state):
        state = step(qi - 1, 0, 0, state, True)
        _, alphas = step(qi, 1, 1, state, False)
        pv(qi, 1, alphas)
        return 0

    def tail_odd(state):
        state = step(qi - 2, 0, None, state, True)
        state = step(qi - 1, 1, 0, state, True)
        _, alphas = step(qi, 0, 1, state, False)
        pv(qi, 0, alphas)
        return 0

    branch = jnp.where(qi == 0, 0, 1 + n_far % 2)
    lax.switch(branch, [tail_first, tail_even, tail_odd], state)

    lam_p = lam_ref[...]
    lam = (jnp.exp(jnp.sum(lam_p[0:1] * lam_p[1:2], axis=1, keepdims=True))
           - jnp.exp(jnp.sum(lam_p[2:3] * lam_p[3:4], axis=1, keepdims=True)) + lambda_init)
    a1 = acc_ref[0]
    a2 = acc_ref[1]
    o = a1[:dv] / a1[dv:dv + 1] - lam * (a2[:dv] / a2[dv:dv + 1])
    y = o * lax.rsqrt(jnp.mean(o * o, axis=0, keepdims=True) + LN_EPS)
    y = y * (g_ref[...] * (1.0 - lambda_init))
    o_ref[0] = y.T.astype(o_ref.dtype)


def _attention(qvt, k3, lam_params, subln_g_col, bias, n_heads, lambda_init):
    b, s, _ = k3.shape
    t = bias.shape[-1]
    dv = ATTN_VALUE_DIM
    nq = s // t
    kern = functools.partial(_attn_kernel, t=t, lambda_init=lambda_init)
    return pl.pallas_call(
        kern,
        grid=(b, n_heads, nq),
        in_specs=[pl.BlockSpec(lam_params.shape, lambda bi, h, qi: (0, 0)),
                  pl.BlockSpec((dv, 1), lambda bi, h, qi: (0, 0)),
                  pl.BlockSpec((dv, t), lambda bi, h, qi: (h, bi * nq + qi)),
                  pl.BlockSpec((1, s, dv), lambda bi, h, qi: (bi, 0, h)),
                  pl.BlockSpec((dv, s), lambda bi, h, qi: (n_heads + h, bi)),
                  pl.BlockSpec((1, 2, t, t), lambda bi, h, qi: (h, 0, 0, 0))],
        out_specs=pl.BlockSpec((1, t, dv), lambda bi, h, qi: (bi, qi, h)),
        out_shape=jax.ShapeDtypeStruct((b, s, n_heads * dv), BF16),
        scratch_shapes=[pltpu.VMEM((dv + ONES_ROWS, s), BF16),
                        pltpu.VMEM((2, t, t), F32), pltpu.VMEM((2, t, t), F32),
                        pltpu.VMEM((2, t, t), BF16), pltpu.VMEM((2, t, t), BF16),
                        pltpu.VMEM((2, dv + ONES_ROWS, t), F32)],
        compiler_params=_cparams(("parallel", "parallel", "arbitrary")),
        name="diff_attention",
    )(lam_params, subln_g_col, qvt, k3, qvt, bias)


CONV_HALO = 32
CONV_ROW_CHUNK = 64


def _conv_kernel(a_ref, g_ref, ah_ref, gh_ref, bglu_ref, w_ref, cb_ref, lng_ref, lnb_ref, o_ref,
                 h_ref, c_ref, *, ts, kw, ch):
    i = pl.program_id(1)
    ba = bglu_ref[:, :ch]
    bg = bglu_ref[:, ch:]

    def glu(a, g):
        return (a.astype(F32) + ba) * jax.nn.sigmoid(g.astype(F32) + bg)

    halo = glu(ah_ref[0], gh_ref[0])
    h_ref[0, 0:CONV_HALO, :] = jnp.where(i > 0, halo, jnp.zeros_like(halo))
    h_ref[0, CONV_HALO:, :] = glu(a_ref[0], g_ref[0])
    lead = CONV_HALO - (kw - 1)
    span = ts + ((lead + kw - 1) // SUBLANES) * SUBLANES
    for b in range(1, SUBLANES):
        n = min(span, CONV_HALO + ts - b)
        h_ref[b, 0:n, :] = h_ref[0, b:b + n, :]

    rc = min(CONV_ROW_CHUNK, ts)
    for c in range(ch // LANES):
        cs = slice(c * LANES, (c + 1) * LANES)
        wc = w_ref[:, cs]
        cb = cb_ref[:, cs]

        accs = [jnp.zeros((rc // SUBLANES, SUBLANES, LANES), F32) for _ in range(0, ts, rc)]
        for k in range(kw):
            a, b = divmod(lead + k, SUBLANES)
            w8 = jnp.broadcast_to(wc[k:k + 1, :], (SUBLANES, LANES))[None]
            for ri, r0 in enumerate(range(0, ts, rc)):
                off = r0 + a * SUBLANES
                accs[ri] = accs[ri] + h_ref[b, off:off + rc, cs].reshape(accs[ri].shape) * w8
        for ri, r0 in enumerate(range(0, ts, rc)):
            c_ref[r0:r0 + rc, cs] = accs[ri].reshape(rc, LANES) + cb

    y = _ln_rows(c_ref[...], lng_ref[...], lnb_ref[...])
    o_ref[0] = (y * jax.nn.sigmoid(y)).astype(o_ref.dtype)


def _conformer_conv(proj3, b_glu, conv_w, conv_b, ln_g, ln_b, glu_col0):
    b, s, _ = proj3.shape
    kw, ch = conv_w.shape
    assert kw - 1 <= CONV_HALO and glu_col0 % ch == 0
    ts = _pick(256, s)
    cb0 = glu_col0 // ch
    hb = ts // CONV_HALO
    kern = functools.partial(_conv_kernel, ts=ts, kw=kw, ch=ch)
    halo_idx = lambda off: (lambda bi, i: (bi, jnp.maximum(i * hb - 1, 0), off))
    vec = lambda n: pl.BlockSpec((1, n), lambda bi, i: (0, 0))
    return pl.pallas_call(
        kern,
        grid=(b, s // ts),
        in_specs=[pl.BlockSpec((1, ts, ch), lambda bi, i: (bi, i, cb0)),
                  pl.BlockSpec((1, ts, ch), lambda bi, i: (bi, i, cb0 + 1)),
                  pl.BlockSpec((1, CONV_HALO, ch), halo_idx(cb0)),
                  pl.BlockSpec((1, CONV_HALO, ch), halo_idx(cb0 + 1)),
                  vec(2 * ch),
                  pl.BlockSpec((kw, ch), lambda bi, i: (0, 0)),
                  vec(ch), vec(ch), vec(ch)],
        out_specs=pl.BlockSpec((1, ts, ch), lambda bi, i: (bi, i, 0)),
        out_shape=jax.ShapeDtypeStruct((b, s, ch), BF16),
        scratch_shapes=[pltpu.VMEM((SUBLANES, CONV_HALO + ts, ch), F32),
                        pltpu.VMEM((ts, ch), F32)],
        compiler_params=_cparams(("parallel", "arbitrary")),
        name="conformer_conv",
    )(proj3, proj3, proj3, proj3, b_glu, conv_w, conv_b, ln_g, ln_b)


def _outproj_kernel(a_ref, c_ref, wa_ref, wc_ref, x_ref, g_ref, b_ref, o_ref, ob_ref, obt_ref):
    mix = (jnp.dot(a_ref[...], wa_ref[...], preferred_element_type=F32)
           + jnp.dot(c_ref[...], wc_ref[...], preferred_element_type=F32))
    y = _ln_rows(DEEPNORM_ALPHA * x_ref[...] + mix, g_ref[...], b_ref[...])
    o_ref[...] = y
    ob_ref[...] = y.astype(BF16)
    obt_ref[...] = y.T.astype(BF16)


def _out_proj_ln(attn2, conv2, w_out, x2, ln_g, ln_b):
    m, d = x2.shape
    da = attn2.shape[1]
    dc = conv2.shape[1]
    tm = _pick(512, m)
    vec = pl.BlockSpec((1, d), lambda i: (0, 0))
    return pl.pallas_call(
        _outproj_kernel,
        grid=(m // tm,),
        in_specs=[pl.BlockSpec((tm, da), lambda i: (i, 0)),
                  pl.BlockSpec((tm, dc), lambda i: (i, 0)),
                  pl.BlockSpec((da, d), lambda i: (0, 0)),
                  pl.BlockSpec((dc, d), lambda i: (0, 0)),
                  pl.BlockSpec((tm, d), lambda i: (i, 0)),
                  vec, vec],
        out_specs=[pl.BlockSpec((tm, d), lambda i: (i, 0)),
                   pl.BlockSpec((tm, d), lambda i: (i, 0)),
                   pl.BlockSpec((d, tm), lambda i: (0, i))],
        out_shape=[jax.ShapeDtypeStruct((m, d), F32), jax.ShapeDtypeStruct((m, d), BF16),
                   jax.ShapeDtypeStruct((d, m), BF16)],
        compiler_params=_cparams(("parallel",)),
        name="out_proj_ln1",
    )(attn2, conv2, w_out[:da], w_out[da:], x2, ln_g, ln_b)


SELECT_UNROLL = 2


def _top_desc(s3, k):
    tops = []
    rank = jnp.full(s3.shape, float(k), F32)
    work = s3
    for a in range(k):
        m = _allmax8(jnp.max(work, axis=0))
        hit = work == m[None]
        rank = jnp.where(hit, float(a), rank)
        work = jnp.where(hit, -jnp.inf, work)
        tops.append(m)
    return tops, rank


def _kth_largest(c3, k):
    work = c3
    m = None
    for _ in range(k):
        m = _allmax8(jnp.max(work, axis=0))
        work = jnp.where(work == m[None], -jnp.inf, work)
    return m


def _rows_of(tops, lo):
    sub = lax.broadcasted_iota(jnp.int32, tops[lo].shape, 0)
    out = tops[lo]
    for r in range(1, SUBLANES):
        out = jnp.where(sub == r, tops[lo + r], out)
    return out


def _peer_select_kernel(x_ref, wq_ref, keys_ref, n_ref, ea_ref, rb_ref, eb_ref, s_ref,
                        *, n_heads, nk, tt):
    k = PEER_TOPK
    q = jnp.dot(x_ref[...], wq_ref[...], preferred_element_type=F32).astype(BF16)
    hd = keys_ref.shape[2]
    for hc in range(2 * n_heads):
        s_ref[hc] = lax.dot_general(keys_ref[hc], q[:, hc * hd:(hc + 1) * hd],
                                    (((1,), (1,)), ((), ())), preferred_element_type=F32)

    def group(h, c0):
        cols = pl.ds(c0, LANES)
        g3 = (nk // SUBLANES, SUBLANES, LANES)
        sa = s_ref[2 * h, :, cols].reshape(g3)
        sb = s_ref[2 * h + 1, :, cols].reshape(g3)
        t1, ra = _top_desc(sa, k)
        t2, rb = _top_desc(sb, k)
        s2_lo = _rows_of(t2, 0)
        cand = jnp.stack([t1[0] + s2_lo, t1[0] + _rows_of(t2, SUBLANES)]
                         + [t1[a] + s2_lo for a in range(1, SUBLANES)]
                         + [_rows_of(t1, SUBLANES) + t2[0]])
        tau = _kth_largest(cand, k)
        sel = cand >= tau[None]
        z = _allsum8(jnp.sum(jnp.where(sel, jnp.exp(cand - (t1[0] + t2[0])[None]), 0.0), axis=0))
        picked = jnp.where(sel, 1.0, 0.0)
        n_i = jnp.where((ra < float(k)) & (sa + t2[0][None] >= tau[None]), 1.0, 0.0)
        for a in range(SUBLANES):
            n_a = _allsum8(picked[0] + picked[1] if a == 0 else picked[a + 1])
            n_i = jnp.where(ra == float(a), n_a[None], n_i)
        n_ref[h, :, cols] = n_i.reshape(nk, LANES)
        ea_ref[h, :, cols] = (jnp.exp(sa - t1[0][None]) / z[None]).reshape(nk, LANES)
        rb_ref[h, :, cols] = rb.reshape(nk, LANES).astype(BF16)
        eb_ref[h, :, cols] = jnp.exp(sb - t2[0][None]).reshape(nk, LANES).astype(BF16)

    per_head = tt // (LANES * SELECT_UNROLL)

    def body(it, carry):
        h = it // per_head
        base = (it % per_head) * (LANES * SELECT_UNROLL)
        for u in range(SELECT_UNROLL):
            group(h, pl.multiple_of(base + u * LANES, LANES))
        return carry

    lax.fori_loop(0, n_heads * per_head, body, 0)


def _peer_select(x1b, wq, keys2):
    m, d = x1b.shape
    hc, nk, hd = keys2.shape
    n_heads = hc // 2
    assert PEER_TOPK == 2 * SUBLANES and nk % SUBLANES == 0
    tt = _pick(512, m)
    assert tt % (LANES * SELECT_UNROLL) == 0
    kern = functools.partial(_peer_select_kernel, n_heads=n_heads, nk=nk, tt=tt)
    out_spec = pl.BlockSpec((n_heads, nk, tt), lambda i: (0, 0, i))
    shp = lambda dt: jax.ShapeDtypeStruct((n_heads, nk, m), dt)
    return pl.pallas_call(
        kern,
        grid=(m // tt,),
        in_specs=[pl.BlockSpec((tt, d), lambda i: (i, 0)),
                  pl.BlockSpec(wq.shape, lambda i: (0, 0)),
                  pl.BlockSpec(keys2.shape, lambda i: (0, 0, 0))],
        out_specs=[out_spec, out_spec, out_spec, out_spec],
        out_shape=[shp(F32), shp(F32), shp(BF16), shp(BF16)],
        scratch_shapes=[pltpu.VMEM((hc, nk, tt), F32)],
        compiler_params=_cparams(("parallel",)),
        name="peer_select",
    )(x1b, wq, keys2)


PEER_STEP_BLOCKS = 8
PEER_SLAB_BLOCKS = 2
PEER_OUT_PARTS = 1


def _peer_dense_kernel(xt_ref, u_ref, vt_ref, n_ref, ea_ref, rb_ref, eb_ref, x_ref, g_ref, b_ref,
                       o_ref, acc_ref, *zw_refs, n_heads, ib, nk, slab_blocks, parts):
    c = pl.program_id(1)
    n_slabs = ib // slab_blocks
    z_refs, w_refs = zw_refs[:n_slabs], zw_refs[n_slabs:]

    @pl.when(c == 0)
    def _():
        acc_ref[...] = jnp.zeros(acc_ref.shape, F32)

    slab = slab_blocks * nk
    piece = (ib // parts) * nk
    g0 = pl.multiple_of(c * ib, ib)
    n_rows = [n_ref[h, pl.ds(g0, ib), :] for h in range(n_heads)]
    ea_rows = [ea_ref[h, pl.ds(g0, ib), :] for h in range(n_heads)]

    def pre_activation(si):
        z_refs[si][...] = jnp.dot(u_ref[si * slab:(si + 1) * slab, :], xt_ref[...],
                                  preferred_element_type=F32)

    def gate(i):
        si, zi = divmod(i * nk, slab)
        z = z_refs[si][zi:zi + nk, :]
        act = 0.5 * z * (1.0 + lax.erf(z * (2.0 ** -0.5)))
        tt = z.shape[1]
        pack = 2 * SUBLANES
        bshape = (nk // pack, pack, tt)

        def row16(rows):
            return jnp.broadcast_to(rows[i:i + 1, :], (pack, tt)).astype(BF16)[None]

        w = jnp.zeros(bshape, BF16)
        for h in range(n_heads):
            eb = eb_ref[h].reshape(bshape)
            hit = rb_ref[h].reshape(bshape) < row16(n_rows[h])
            w = w + jnp.where(hit, eb, jnp.zeros_like(eb)) * row16(ea_rows[h])
        pi, wi = divmod(i * nk, piece)
        w_refs[pi][wi:wi + nk, :] = act.astype(BF16) * w.reshape(nk, tt)

    pre_activation(0)
    if n_slabs > 1:
        pre_activation(1)
    pv = None
    for i in range(ib):
        si = (i * nk) // slab
        if (i * nk) % slab == 0 and si + 2 < n_slabs:
            pre_activation(si + 2)
        gate(i)
        if (i + 1) * nk % piece == 0:
            pi = (i * nk) // piece
            part = jnp.dot(vt_ref[0, :, pi * piece:(pi + 1) * piece], w_refs[pi][...],
                           preferred_element_type=F32)
            pv = part if pv is None else pv + part
    acc_ref[...] += pv

    @pl.when(c == pl.num_programs(1) - 1)
    def _():
        y = DEEPNORM_ALPHA * x_ref[...] + acc_ref[...].T
        o_ref[...] = _ln_rows(y, g_ref[...], b_ref[...])


def _peer_dense(x1bt, x1, u, v, sel, ln_g, ln_b, nk, slab_blocks=PEER_SLAB_BLOCKS,
                parts=PEER_OUT_PARTS, name="peer_experts_ln2"):
    m, d = x1.shape
    ne = u.shape[0]
    n_i, ea, rb, eb = sel
    n_heads = n_i.shape[0]
    tt = _pick(512, m)
    ib = PEER_STEP_BLOCKS
    ec = ib * nk
    vt = v.reshape(ne // ec, ec, d).transpose(0, 2, 1)
    kern = functools.partial(_peer_dense_kernel, n_heads=n_heads, ib=ib, nk=nk,
                             slab_blocks=slab_blocks, parts=parts)
    once = pl.Buffered(1)
    sel_spec = pl.BlockSpec((n_heads, nk, tt), lambda t, c: (0, 0, t), pipeline_mode=once)
    vec = pl.BlockSpec((1, d), lambda t, c: (0, 0))
    return pl.pallas_call(
        kern,
        grid=(m // tt, ne // ec),
        in_specs=[pl.BlockSpec((d, tt), lambda t, c: (0, t)),
                  pl.BlockSpec((ec, d), lambda t, c: (c, 0)),
                  pl.BlockSpec((1, d, ec), lambda t, c: (c, 0, 0)),
                  sel_spec, sel_spec, sel_spec, sel_spec,
                  pl.BlockSpec((tt, d), lambda t, c: (t, 0), pipeline_mode=once),
                  vec, vec],
        out_specs=pl.BlockSpec((tt, d), lambda t, c: (t, 0)),
        out_shape=jax.ShapeDtypeStruct((m, d), F32),
        scratch_shapes=([pltpu.VMEM((d, tt), F32)]
                        + [pltpu.VMEM((slab_blocks * nk, tt), F32)] * (ib // slab_blocks)
                        + [pltpu.VMEM((ec // parts, tt), BF16)] * parts),
        compiler_params=_cparams(("parallel", "arbitrary")),
        name=name,
    )(x1bt, u, vt, n_i, ea, rb, eb, x1, ln_g, ln_b)


def kernel(x, w_in, b_glu, conv_w, conv_b, conv_ln_g, conv_ln_b, lambda_q1, lambda_k1, lambda_q2,
           lambda_k2, subln_g, rel_bias, w_out, ln1_g, ln1_b, peer_wq, peer_keys, peer_u, peer_v,
           ln2_g, ln2_b):
    b, s, d = x.shape
    n_layers = w_in.shape[0]
    n_heads = rel_bias.shape[1]
    qk_cols = n_heads * 2 * ATTN_HEAD_DIM
    v_cols = n_heads * ATTN_VALUE_DIM
    t_attn = _pick(512, s)
    bias = _bias_tiles(rel_bias, t_attn)
    row = lambda v: v.reshape(1, -1).astype(F32)

    x2 = x.reshape(b * s, d)
    for layer in range(n_layers):
        lambda_init = 0.8 - 0.6 * math.exp(-0.3 * layer)
        w = w_in[layer].astype(BF16)
        w_q, w_k, w_v, w_glu = (w[:, :qk_cols], w[:, qk_cols:2 * qk_cols],
                                w[:, 2 * qk_cols:2 * qk_cols + v_cols], w[:, 2 * qk_cols + v_cols:])
        kg3 = _in_proj(x2, jnp.concatenate([w_k, w_glu], axis=1), False).reshape(b, s, -1)
        qvt = _in_proj(x2, jnp.concatenate([w_q, w_v], axis=1).T, True)
        lam_params = jnp.stack([lambda_q1[layer], lambda_k1[layer],
                                lambda_q2[layer], lambda_k2[layer]]).astype(F32)
        attn = _attention(qvt, kg3, lam_params, subln_g[layer].reshape(-1, 1).astype(F32), bias,
                          n_heads, lambda_init)
        conv = _conformer_conv(kg3, row(b_glu[layer]), conv_w[layer].astype(F32),
                               row(conv_b[layer]), row(conv_ln_g[layer]), row(conv_ln_b[layer]),
                               qk_cols)
        x1, x1b, x1bt = _out_proj_ln(attn.reshape(b * s, -1), conv.reshape(b * s, -1),
                                     w_out[layer].astype(BF16), x2, row(ln1_g[layer]),
                                     row(ln1_b[layer]))
        keys = peer_keys[layer]
        ph, _, nk, hd = keys.shape
        sel = _peer_select(x1b, peer_wq[layer].astype(BF16),
                           keys.reshape(2 * ph, nk, hd).astype(BF16))
        x2 = _peer_dense(x1bt, x1, peer_u[layer].astype(BF16), peer_v[layer].astype(BF16), sel,
                         row(ln2_g[layer]), row(ln2_b[layer]), nk)
    return x2.reshape(b, s, d)
```

```python
import functools
import math

import numpy as np

import jax
import jax.numpy as jnp
from jax import lax
from jax.experimental import pallas as pl
from jax.experimental.pallas import tpu as pltpu

F32 = jnp.float32
BF16 = jnp.bfloat16

ATTN_HEAD_DIM = 64
ATTN_VALUE_DIM = 2 * ATTN_HEAD_DIM
REL_MAX_DISTANCE = 128
PEER_TOPK = 16
LN_EPS = 1e-5
DEPTH = 1
DEEPNORM_ALPHA = (2.0 * DEPTH) ** 0.25

LANES = 128
SUBLANES = 8
VMEM_LIMIT_BYTES = 56 * 1024 * 1024


def _pick(limit, total, quantum=128):
    if total <= limit:
        return total
    t = (limit // quantum) * quantum
    while t > quantum and total % t:
        t -= quantum
    assert total % t == 0, (limit, total)
    return t


def _cparams(sem):
    return pltpu.CompilerParams(dimension_semantics=sem, vmem_limit_bytes=VMEM_LIMIT_BYTES)


def _allmax8(x):
    for sh in (4, 2, 1):
        x = jnp.maximum(x, pltpu.roll(x, sh, 0))
    return x


def _allsum8(x):
    for sh in (4, 2, 1):
        x = x + pltpu.roll(x, sh, 0)
    return x


def _ln_rows(y, g, b):
    mu = jnp.mean(y, axis=-1, keepdims=True)
    d = y - mu
    var = jnp.mean(d * d, axis=-1, keepdims=True)
    return d * lax.rsqrt(var + LN_EPS) * g + b


def _inproj_kernel(x_ref, w_ref, o_ref, xb_ref, *, transposed):
    @pl.when(pl.program_id(1) == 0)
    def _():
        xb_ref[...] = x_ref[...].astype(BF16)

    if transposed:
        o = lax.dot_general(w_ref[...], xb_ref[...], (((1,), (1,)), ((), ())),
                            preferred_element_type=F32)
    else:
        o = jnp.dot(xb_ref[...], w_ref[...], preferred_element_type=F32)
    o_ref[...] = o.astype(o_ref.dtype)


def _in_proj(x2, w, transposed, tm_lim=1024, tn_lim=1024, name=None):
    m, d = x2.shape
    n = w.shape[0] if transposed else w.shape[1]
    tm = _pick(tm_lim, m)
    tn = _pick(tn_lim, n)
    if transposed:
        w_spec = pl.BlockSpec((tn, d), lambda i, j: (j, 0))
        o_spec = pl.BlockSpec((tn, tm), lambda i, j: (j, i))
        o_shape = jax.ShapeDtypeStruct((n, m), BF16)
    else:
        w_spec = pl.BlockSpec((d, tn), lambda i, j: (0, j))
        o_spec = pl.BlockSpec((tm, tn), lambda i, j: (i, j))
        o_shape = jax.ShapeDtypeStruct((m, n), BF16)
    return pl.pallas_call(
        functools.partial(_inproj_kernel, transposed=transposed),
        grid=(m // tm, n // tn),
        in_specs=[pl.BlockSpec((tm, d), lambda i, j: (i, 0)), w_spec],
        out_specs=o_spec,
        out_shape=o_shape,
        scratch_shapes=[pltpu.VMEM((tm, d), BF16)],
        compiler_params=_cparams(("parallel", "arbitrary")),
        name=name or ("in_proj_t" if transposed else "in_proj"),
    )(x2, w)


LOG2E = math.log2(math.e)
ONES_ROWS = 16


def _t5_bucket_np(dist, num_buckets):
    max_exact = num_buckets // 2
    n_f = np.maximum(dist, 1).astype(np.float32)
    large = max_exact + (np.log(n_f / np.float32(max_exact))
                         / np.float32(math.log(REL_MAX_DISTANCE / max_exact))
                         * np.float32(num_buckets - max_exact)).astype(np.int32)
    large = np.minimum(large, num_buckets - 1)
    return np.where(dist < max_exact, dist, large).astype(np.int32)


def _bias_kernel(rb_ref, bucket_ref, o_ref, *, nb):
    h = pl.program_id(0)
    bucket = bucket_ref[...]
    tile = jnp.zeros(bucket.shape, F32)
    for bkt in range(nb):
        tile = jnp.where(bucket == bkt, rb_ref[bkt, h], tile)
    tile = (tile - rb_ref[nb - 1, h]) * LOG2E
    o_ref[0] = jnp.where(bucket < 0, -jnp.inf, tile)


def _bias_tiles(rel_bias, t):
    nb, n_heads = rel_bias.shape
    assert t + 1 >= REL_MAX_DISTANCE
    key = np.arange(t, dtype=np.int32)[:, None]
    qry = np.arange(t, dtype=np.int32)[None, :]
    d_diag = qry - key
    diag = np.where(d_diag >= 0, _t5_bucket_np(np.maximum(d_diag, 0), nb), -1)
    prev = _t5_bucket_np(d_diag + t, nb)
    buckets = jnp.asarray(np.stack([prev, diag]).astype(np.int32))
    return pl.pallas_call(
        functools.partial(_bias_kernel, nb=nb),
        grid=(n_heads,),
        in_specs=[pl.BlockSpec(memory_space=pltpu.SMEM),
                  pl.BlockSpec((2, t, t), lambda h: (0, 0, 0))],
        out_specs=pl.BlockSpec((1, 2, t, t), lambda h: (h, 0, 0, 0)),
        out_shape=jax.ShapeDtypeStruct((n_heads, 2, t, t), F32),
        compiler_params=_cparams(("arbitrary",)),
        name="rel_bias_tiles",
    )(rel_bias.astype(F32), buckets)


def _attn_kernel(lam_ref, g_ref, qt_ref, k_ref, vt_ref, bias_ref, o_ref, vext_ref, sa_ref, sb_ref,
                    pa_ref, pb_ref, acc_ref, *, t, lambda_init):
    qi = pl.program_id(2)
    dv = ATTN_VALUE_DIM

    @pl.when(qi == 0)
    def _():
        vext_ref[0:dv, :] = vt_ref[...]
        vext_ref[dv:, :] = jnp.ones((ONES_ROWS, vext_ref.shape[1]), BF16)

    qf = qt_ref[...].astype(F32) * (ATTN_HEAD_DIM ** -0.5 * LOG2E)
    row = lax.broadcasted_iota(jnp.int32, qf.shape, 0)
    comps = (jnp.where(row < ATTN_HEAD_DIM, qf, 0.0).astype(BF16),
             jnp.where(row >= ATTN_HEAD_DIM, qf, 0.0).astype(BF16))
    s_bufs = (sa_ref, sb_ref)
    p_bufs = (pa_ref, pb_ref)

    def qk(j, buf):
        kj = k_ref[0, pl.ds(pl.multiple_of(j * t, t), t), :]
        for c in range(2):
            s_bufs[buf][c] = jnp.dot(kj, comps[c], preferred_element_type=F32)

    def pv(j, buf, alphas):
        vj = vext_ref[:, pl.ds(pl.multiple_of(j * t, t), t)]
        for c in range(2):
            acc3 = acc_ref[c].reshape(-1, SUBLANES, t) * alphas[c][None]
            acc_ref[c] = (acc3.reshape(acc_ref.shape[1:])
                          + jnp.dot(vj, p_bufs[buf][c], preferred_element_type=F32))

    def step(j, buf, bias_idx, state, prefetch):
        ms, alphas = state
        if prefetch:
            qk(j + 1, 1 - buf)
        pv(jnp.maximum(j - 1, 0), 1 - buf, alphas)
        s_ref, p_ref = s_bufs[buf], p_bufs[buf]

        def rows(c, r0, n):
            s = s_ref[c, r0:r0 + n, :]
            return s if bias_idx is None else s + bias_ref[0, bias_idx, r0:r0 + n, :]

        out = []
        for c in range(2):
            m8 = rows(c, 0, SUBLANES)
            for r0 in range(SUBLANES, t, SUBLANES):
                m8 = jnp.maximum(m8, rows(c, r0, SUBLANES))
            m_new = jnp.maximum(ms[c], _allmax8(m8))
            m16 = jnp.concatenate([m_new, m_new], axis=0)
            for r0 in range(0, t, 2 * SUBLANES):
                p_ref[c, r0:r0 + 2 * SUBLANES, :] = jnp.exp2(rows(c, r0, 2 * SUBLANES) - m16).astype(BF16)
            out.append((m_new, jnp.exp2(ms[c] - m_new)))
        return tuple(m for m, _ in out), tuple(a for _, a in out)

    acc_ref[...] = jnp.zeros(acc_ref.shape, F32)
    pb_ref[...] = jnp.zeros(pb_ref.shape, BF16)
    m0 = jnp.full((SUBLANES, t), -jnp.inf, F32)
    one = jnp.ones((SUBLANES, t), F32)
    qk(0, 0)
    n_far = jnp.maximum(qi - 1, 0)

    def pair(i, state):
        state = step(2 * i, 0, None, state, True)
        return step(2 * i + 1, 1, None, state, True)

    state = lax.fori_loop(0, n_far // 2, pair, ((m0, m0), (one, one)))

    def tail_first(state):
        _, alphas = step(qi, 0, 1, state, False)
        pv(qi, 0, alphas)
        return 0

    def tail_even(state):
        state = step(qi - 1, 0, 0, state, True)
        _, alphas = step(qi, 1, 1, state, False)
        pv(qi, 1, alphas)
        return 0

    def tail_odd(state):
        state = step(qi - 2, 0, None, state, True)
        state = step(qi - 1, 1, 0, state, True)
        _, alphas = step(qi, 0, 1, state, False)
        pv(qi, 0, alphas)
        return 0

    branch = jnp.where(qi == 0, 0, 1 + n_far % 2)
    lax.switch(branch, [tail_first, tail_even, tail_odd], state)

    lam_p = lam_ref[...]
    lam = (jnp.exp(jnp.sum(lam_p[0:1] * lam_p[1:2], axis=1, keepdims=True))
           - jnp.exp(jnp.sum(lam_p[2:3] * lam_p[3:4], axis=1, keepdims=True)) + lambda_init)
    a1 = acc_ref[0]
    a2 = acc_ref[1]
    o = a1[:dv] / a1[dv:dv + 1] - lam * (a2[:dv] / a2[dv:dv + 1])
    y = o * lax.rsqrt(jnp.mean(o * o, axis=0, keepdims=True) + LN_EPS)
    y = y * (g_ref[...] * (1.0 - lambda_init))
    o_ref[0] = y.T.astype(o_ref.dtype)


def _attention(qvt, k3, lam_params, subln_g_col, bias, n_heads, lambda_init):
    b, s, _ = k3.shape
    t = bias.shape[-1]
    dv = ATTN_VALUE_DIM
    nq = s // t
    kern = functools.partial(_attn_kernel, t=t, lambda_init=lambda_init)
    return pl.pallas_call(
        kern,
        grid=(b, n_heads, nq),
        in_specs=[pl.BlockSpec(lam_params.shape, lambda bi, h, qi: (0, 0)),
                  pl.BlockSpec((dv, 1), lambda bi, h, qi: (0, 0)),
                  pl.BlockSpec((dv, t), lambda bi, h, qi: (h, bi * nq + qi)),
                  pl.BlockSpec((1, s, dv), lambda bi, h, qi: (bi, 0, h)),
                  pl.BlockSpec((dv, s), lambda bi, h, qi: (n_heads + h, bi)),
                  pl.BlockSpec((1, 2, t, t), lambda bi, h, qi: (h, 0, 0, 0))],
        out_specs=pl.BlockSpec((1, t, dv), lambda bi, h, qi: (bi, qi, h)),
        out_shape=jax.ShapeDtypeStruct((b, s, n_heads * dv), BF16),
        scratch_shapes=[pltpu.VMEM((dv + ONES_ROWS, s), BF16),
                        pltpu.VMEM((2, t, t), F32), pltpu.VMEM((2, t, t), F32),
                        pltpu.VMEM((2, t, t), BF16), pltpu.VMEM((2, t, t), BF16),
                        pltpu.VMEM((2, dv + ONES_ROWS, t), F32)],
        compiler_params=_cparams(("parallel", "parallel", "arbitrary")),
        name="diff_attention",
    )(lam_params, subln_g_col, qvt, k3, qvt, bias)


CONV_HALO = 32
CONV_ROW_CHUNK = 64


def _conv_kernel(a_ref, g_ref, ah_ref, gh_ref, bglu_ref, w_ref, cb_ref, lng_ref, lnb_ref, o_ref,
                 h_ref, c_ref, *, ts, kw, ch, rc):
    i = pl.program_id(1)
    ba = bglu_ref[:, :ch]
    bg = bglu_ref[:, ch:]

    def glu(a, g):
        return (a.astype(F32) + ba) * jax.nn.sigmoid(g.astype(F32) + bg)

    halo = glu(ah_ref[0], gh_ref[0])
    h_ref[0, 0:CONV_HALO, :] = jnp.where(i > 0, halo, jnp.zeros_like(halo))
    h_ref[0, CONV_HALO:, :] = glu(a_ref[0], g_ref[0])
    lead = CONV_HALO - (kw - 1)
    span = ts + ((lead + kw - 1) // SUBLANES) * SUBLANES
    for b in range(1, SUBLANES):
        n = min(span, CONV_HALO + ts - b)
        h_ref[b, 0:n, :] = h_ref[0, b:b + n, :]

    for c in range(ch // LANES):
        cs = slice(c * LANES, (c + 1) * LANES)
        wc = w_ref[:, cs]
        cb = cb_ref[:, cs]

        accs = [jnp.zeros((rc // SUBLANES, SUBLANES, LANES), F32) for _ in range(0, ts, rc)]
        for k in range(kw):
            a, b = divmod(lead + k, SUBLANES)
            w8 = jnp.broadcast_to(wc[k:k + 1, :], (SUBLANES, LANES))[None]
            for ri, r0 in enumerate(range(0, ts, rc)):
                off = r0 + a * SUBLANES
                accs[ri] = accs[ri] + h_ref[b, off:off + rc, cs].reshape(accs[ri].shape) * w8
        for ri, r0 in enumerate(range(0, ts, rc)):
            c_ref[r0:r0 + rc, cs] = accs[ri].reshape(rc, LANES) + cb

    y = _ln_rows(c_ref[...], lng_ref[...], lnb_ref[...])
    o_ref[0] = (y * jax.nn.sigmoid(y)).astype(o_ref.dtype)


def _conformer_conv(proj3, b_glu, conv_w, conv_b, ln_g, ln_b, glu_col0, ts_lim=256,
                    rc=CONV_ROW_CHUNK, name="conformer_conv"):
    b, s, _ = proj3.shape
    kw, ch = conv_w.shape
    assert kw - 1 <= CONV_HALO and glu_col0 % ch == 0
    ts = _pick(ts_lim, s)
    cb0 = glu_col0 // ch
    hb = ts // CONV_HALO
    kern = functools.partial(_conv_kernel, ts=ts, kw=kw, ch=ch, rc=min(rc, ts))
    halo_idx = lambda off: (lambda bi, i: (bi, jnp.maximum(i * hb - 1, 0), off))
    vec = lambda n: pl.BlockSpec((1, n), lambda bi, i: (0, 0))
    return pl.pallas_call(
        kern,
        grid=(b, s // ts),
        in_specs=[pl.BlockSpec((1, ts, ch), lambda bi, i: (bi, i, cb0)),
                  pl.BlockSpec((1, ts, ch), lambda bi, i: (bi, i, cb0 + 1)),
                  pl.BlockSpec((1, CONV_HALO, ch), halo_idx(cb0)),
                  pl.BlockSpec((1, CONV_HALO, ch), halo_idx(cb0 + 1)),
                  vec(2 * ch),
                  pl.BlockSpec((kw, ch), lambda bi, i: (0, 0)),
                  vec(ch), vec(ch), vec(ch)],
        out_specs=pl.BlockSpec((1, ts, ch), lambda bi, i: (bi, i, 0)),
        out_shape=jax.ShapeDtypeStruct((b, s, ch), BF16),
        scratch_shapes=[pltpu.VMEM((SUBLANES, CONV_HALO + ts, ch), F32),
                        pltpu.VMEM((ts, ch), F32)],
        compiler_params=_cparams(("parallel", "arbitrary")),
        name=name,
    )(proj3, proj3, proj3, proj3, b_glu, conv_w, conv_b, ln_g, ln_b)


def _outproj_kernel(a_ref, c_ref, wa_ref, wc_ref, x_ref, g_ref, b_ref, o_ref, ob_ref, obt_ref):
    mix = (jnp.dot(a_ref[...], wa_ref[...], preferred_element_type=F32)
           + jnp.dot(c_ref[...], wc_ref[...], preferred_element_type=F32))
    y = _ln_rows(DEEPNORM_ALPHA * x_ref[...] + mix, g_ref[...], b_ref[...])
    o_ref[...] = y
    ob_ref[...] = y.astype(BF16)
    obt_ref[...] = y.T.astype(BF16)


def _out_proj_ln(attn2, conv2, w_out, x2, ln_g, ln_b, tm_lim=512, name="out_proj_ln1"):
    m, d = x2.shape
    da = attn2.shape[1]
    dc = conv2.shape[1]
    tm = _pick(tm_lim, m)
    vec = pl.BlockSpec((1, d), lambda i: (0, 0))
    return pl.pallas_call(
        _outproj_kernel,
        grid=(m // tm,),
        in_specs=[pl.BlockSpec((tm, da), lambda i: (i, 0)),
                  pl.BlockSpec((tm, dc), lambda i: (i, 0)),
                  pl.BlockSpec((da, d), lambda i: (0, 0)),
                  pl.BlockSpec((dc, d), lambda i: (0, 0)),
                  pl.BlockSpec((tm, d), lambda i: (i, 0)),
                  vec, vec],
        out_specs=[pl.BlockSpec((tm, d), lambda i: (i, 0)),
                   pl.BlockSpec((tm, d), lambda i: (i, 0)),
                   pl.BlockSpec((d, tm), lambda i: (0, i))],
        out_shape=[jax.ShapeDtypeStruct((m, d), F32), jax.ShapeDtypeStruct((m, d), BF16),
                   jax.ShapeDtypeStruct((d, m), BF16)],
        compiler_params=_cparams(("parallel",)),
        name=name,
    )(attn2, conv2, w_out[:da], w_out[da:], x2, ln_g, ln_b)


SELECT_UNROLL = 2


def _top_desc(s3, k):
    tops = []
    rank = jnp.full(s3.shape, float(k), F32)
    work = s3
    for a in range(k):
        m = _allmax8(jnp.max(work, axis=0))
        hit = work == m[None]
        rank = jnp.where(hit, float(a), rank)
        work = jnp.where(hit, -jnp.inf, work)
        tops.append(m)
    return tops, rank


def _kth_largest(c3, k):
    work = c3
    m = None
    for _ in range(k):
        m = _allmax8(jnp.max(work, axis=0))
        work = jnp.where(work == m[None], -jnp.inf, work)
    return m


def _rows_of(tops, lo):
    sub = lax.broadcasted_iota(jnp.int32, tops[lo].shape, 0)
    out = tops[lo]
    for r in range(1, SUBLANES):
        out = jnp.where(sub == r, tops[lo + r], out)
    return out


def _peer_select_kernel(x_ref, wq_ref, keys_ref, nea_ref, rb_ref, eb_ref, s_ref,
                        *, n_heads, nk, tt, unroll):
    k = PEER_TOPK
    q = jnp.dot(x_ref[...], wq_ref[...], preferred_element_type=F32).astype(BF16)
    hd = keys_ref.shape[2]
    for hc in range(2 * n_heads):
        s_ref[hc] = lax.dot_general(keys_ref[hc], q[:, hc * hd:(hc + 1) * hd],
                                    (((1,), (1,)), ((), ())), preferred_element_type=F32)

    def group(h, c0):
        cols = pl.ds(c0, LANES)
        g3 = (nk // SUBLANES, SUBLANES, LANES)
        sa = s_ref[2 * h, :, cols].reshape(g3)
        sb = s_ref[2 * h + 1, :, cols].reshape(g3)
        t1, ra = _top_desc(sa, k)
        t2, rb = _top_desc(sb, k)
        s2_lo = _rows_of(t2, 0)
        cand = jnp.stack([t1[0] + s2_lo, t1[0] + _rows_of(t2, SUBLANES)]
                         + [t1[a] + s2_lo for a in range(1, SUBLANES)]
                         + [_rows_of(t1, SUBLANES) + t2[0]])
        tau = _kth_largest(cand, k)
        sel = cand >= tau[None]
        z = _allsum8(jnp.sum(jnp.where(sel, jnp.exp(cand - (t1[0] + t2[0])[None]), 0.0), axis=0))
        picked = jnp.where(sel, 1.0, 0.0)
        n_i = jnp.where((ra < float(k)) & (sa + t2[0][None] >= tau[None]), 1.0, 0.0)
        for a in range(SUBLANES):
            n_a = _allsum8(picked[0] + picked[1] if a == 0 else picked[a + 1])
            n_i = jnp.where(ra == float(a), n_a[None], n_i)
        ea = jnp.exp(sa - t1[0][None]) / z[None]
        nea_ref[h, :, cols] = (2.0 * n_i + ea).reshape(nk, LANES)
        rb_ref[h, :, cols] = rb.reshape(nk, LANES).astype(BF16)
        eb_ref[h, :, cols] = jnp.exp(sb - t2[0][None]).reshape(nk, LANES).astype(BF16)

    per_head = tt // (LANES * unroll)

    def body(it, carry):
        h = it // per_head
        base = (it % per_head) * (LANES * unroll)
        for u in range(unroll):
            group(h, pl.multiple_of(base + u * LANES, LANES))
        return carry

    lax.fori_loop(0, n_heads * per_head, body, 0)


def _peer_select(x1b, wq, keys2, unroll=SELECT_UNROLL, tt_lim=512, name="peer_select"):
    m, d = x1b.shape
    hc, nk, hd = keys2.shape
    n_heads = hc // 2
    assert PEER_TOPK == 2 * SUBLANES and nk % SUBLANES == 0
    tt = _pick(tt_lim, m)
    assert tt % (LANES * unroll) == 0
    kern = functools.partial(_peer_select_kernel, n_heads=n_heads, nk=nk, tt=tt, unroll=unroll)
    out_spec = pl.BlockSpec((n_heads, nk, tt), lambda i: (0, 0, i))
    shp = lambda dt: jax.ShapeDtypeStruct((n_heads, nk, m), dt)
    return pl.pallas_call(
        kern,
        grid=(m // tt,),
        in_specs=[pl.BlockSpec((tt, d), lambda i: (i, 0)),
                  pl.BlockSpec(wq.shape, lambda i: (0, 0)),
                  pl.BlockSpec(keys2.shape, lambda i: (0, 0, 0))],
        out_specs=[out_spec, out_spec, out_spec],
        out_shape=[shp(F32), shp(BF16), shp(BF16)],
        scratch_shapes=[pltpu.VMEM((hc, nk, tt), F32)],
        compiler_params=_cparams(("parallel",)),
        name=name,
    )(x1b, wq, keys2)


PEER_STEP_BLOCKS = 8
PEER_SLAB_BLOCKS = 2
PEER_OUT_PARTS = 1


def _peer_dense_kernel(xt_ref, u_ref, vt_ref, nea_ref, rb_ref, eb_ref, x_ref, g_ref, b_ref,
                       o_ref, acc_ref, *zw_refs, n_heads, ib, nk, slab_blocks, parts):
    c = pl.program_id(1)
    n_slabs = ib // slab_blocks
    z_refs, w_refs = zw_refs[:n_slabs], zw_refs[n_slabs:]

    @pl.when(c == 0)
    def _():
        acc_ref[...] = jnp.zeros(acc_ref.shape, F32)

    slab = slab_blocks * nk
    piece = (ib // parts) * nk
    g0 = pl.multiple_of(c * ib, ib)
    nea = [nea_ref[h, pl.ds(g0, ib), :] for h in range(n_heads)]
    n_rows = [jnp.floor(0.5 * v) for v in nea]
    ea_rows = [v - 2.0 * n for v, n in zip(nea, n_rows)]

    def pre_activation(si):
        z_refs[si][...] = jnp.dot(u_ref[si * slab:(si + 1) * slab, :], xt_ref[...],
                                  preferred_element_type=F32)

    def gate(i):
        si, zi = divmod(i * nk, slab)
        z = z_refs[si][zi:zi + nk, :]
        act = 0.5 * z * (1.0 + lax.erf(z * (2.0 ** -0.5)))
        tt = z.shape[1]
        pack = 2 * SUBLANES
        bshape = (nk // pack, pack, tt)

        def row16(rows):
            return jnp.broadcast_to(rows[i:i + 1, :], (pack, tt)).astype(BF16)[None]

        w = jnp.zeros(bshape, BF16)
        for h in range(n_heads):
            eb = eb_ref[h].reshape(bshape)
            hit = rb_ref[h].reshape(bshape) < row16(n_rows[h])
            w = w + jnp.where(hit, eb, jnp.zeros_like(eb)) * row16(ea_rows[h])
        pi, wi = divmod(i * nk, piece)
        w_refs[pi][wi:wi + nk, :] = act.astype(BF16) * w.reshape(nk, tt)

    pre_activation(0)
    if n_slabs > 1:
        pre_activation(1)
    pv = None
    for i in range(ib):
        si = (i * nk) // slab
        if (i * nk) % slab == 0 and si + 2 < n_slabs:
            pre_activation(si + 2)
        gate(i)
        if (i + 1) * nk % piece == 0:
            pi = (i * nk) // piece
            part = jnp.dot(vt_ref[0, :, pi * piece:(pi + 1) * piece], w_refs[pi][...],
                           preferred_element_type=F32)
            pv = part if pv is None else pv + part
    acc_ref[...] += pv

    @pl.when(c == pl.num_programs(1) - 1)
    def _():
        y = DEEPNORM_ALPHA * x_ref[...] + acc_ref[...].T
        o_ref[...] = _ln_rows(y, g_ref[...], b_ref[...])


def _peer_dense(x1bt, x1, u, v, sel, ln_g, ln_b, nk, ib=PEER_STEP_BLOCKS,
                slab_blocks=PEER_SLAB_BLOCKS, parts=PEER_OUT_PARTS, name="peer_experts_ln2"):
    m, d = x1.shape
    ne = u.shape[0]
    nea, rb, eb = sel
    n_heads = nea.shape[0]
    tt = _pick(512, m)
    ec = ib * nk
    vt = v.reshape(ne // ec, ec, d).transpose(0, 2, 1)
    kern = functools.partial(_peer_dense_kernel, n_heads=n_heads, ib=ib, nk=nk,
                             slab_blocks=slab_blocks, parts=parts)
    once = pl.Buffered(1)
    sel_spec = pl.BlockSpec((n_heads, nk, tt), lambda t, c: (0, 0, t))
    vec = pl.BlockSpec((1, d), lambda t, c: (0, 0))
    return pl.pallas_call(
        kern,
        grid=(m // tt, ne // ec),
        in_specs=[pl.BlockSpec((d, tt), lambda t, c: (0, t)),
                  pl.BlockSpec((ec, d), lambda t, c: (c, 0)),
                  pl.BlockSpec((1, d, ec), lambda t, c: (c, 0, 0)),
                  sel_spec, sel_spec, sel_spec,
                  pl.BlockSpec((tt, d), lambda t, c: (t, 0), pipeline_mode=once),
                  vec, vec],
        out_specs=pl.BlockSpec((tt, d), lambda t, c: (t, 0)),
        out_shape=jax.ShapeDtypeStruct((m, d), F32),
        scratch_shapes=([pltpu.VMEM((d, tt), F32)]
                        + [pltpu.VMEM((slab_blocks * nk, tt), F32)] * (ib // slab_blocks)
                        + [pltpu.VMEM((ec // parts, tt), BF16)] * parts),
        compiler_params=_cparams(("parallel", "arbitrary")),
        name=name,
    )(x1bt, u, vt, nea, rb, eb, x1, ln_g, ln_b)


def kernel(x, w_in, b_glu, conv_w, conv_b, conv_ln_g, conv_ln_b, lambda_q1, lambda_k1, lambda_q2,
           lambda_k2, subln_g, rel_bias, w_out, ln1_g, ln1_b, peer_wq, peer_keys, peer_u, peer_v,
           ln2_g, ln2_b):
    b, s, d = x.shape
    n_layers = w_in.shape[0]
    n_heads = rel_bias.shape[1]
    qk_cols = n_heads * 2 * ATTN_HEAD_DIM
    v_cols = n_heads * ATTN_VALUE_DIM
    t_attn = _pick(512, s)
    bias = _bias_tiles(rel_bias, t_attn)
    row = lambda v: v.reshape(1, -1).astype(F32)

    x2 = x.reshape(b * s, d)
    for layer in range(n_layers):
        lambda_init = 0.8 - 0.6 * math.exp(-0.3 * layer)
        w = w_in[layer].astype(BF16)
        w_q, w_k, w_v, w_glu = (w[:, :qk_cols], w[:, qk_cols:2 * qk_cols],
                                w[:, 2 * qk_cols:2 * qk_cols + v_cols], w[:, 2 * qk_cols + v_cols:])
        kg3 = _in_proj(x2, jnp.concatenate([w_k, w_glu], axis=1), False,
                       tm_lim=1024, tn_lim=1536).reshape(b, s, -1)
        qvt = _in_proj(x2, jnp.concatenate([w_q, w_v], axis=1).T, True, tm_lim=512, tn_lim=2048)
        lam_params = jnp.stack([lambda_q1[layer], lambda_k1[layer],
                                lambda_q2[layer], lambda_k2[layer]]).astype(F32)
        attn = _attention(qvt, kg3, lam_params, subln_g[layer].reshape(-1, 1).astype(F32), bias,
                          n_heads, lambda_init)
        conv = _conformer_conv(kg3, row(b_glu[layer]), conv_w[layer].astype(F32),
                               row(conv_b[layer]), row(conv_ln_g[layer]), row(conv_ln_b[layer]),
                               qk_cols)
        x1, x1b, x1bt = _out_proj_ln(attn.reshape(b * s, -1), conv.reshape(b * s, -1),
                                     w_out[layer].astype(BF16), x2, row(ln1_g[layer]),
                                     row(ln1_b[layer]))
        keys = peer_keys[layer]
        ph, _, nk, hd = keys.shape
        sel = _peer_select(x1b, peer_wq[layer].astype(BF16),
                           keys.reshape(2 * ph, nk, hd).astype(BF16))
        x2 = _peer_dense(x1bt, x1, peer_u[layer].astype(BF16), peer_v[layer].astype(BF16), sel,
                         row(ln2_g[layer]), row(ln2_b[layer]), nk)
    return x2.reshape(b, s, d)
```

```python
import functools
import math

import numpy as np

import jax
import jax.numpy as jnp
from jax import lax
from jax.experimental import pallas as pl
from jax.experimental.pallas import tpu as pltpu

F32 = jnp.float32
BF16 = jnp.bfloat16

ATTN_HEAD_DIM = 64
ATTN_VALUE_DIM = 2 * ATTN_HEAD_DIM
REL_MAX_DISTANCE = 128
PEER_TOPK = 16
LN_EPS = 1e-5
DEPTH = 1
DEEPNORM_ALPHA = (2.0 * DEPTH) ** 0.25

LANES = 128
SUBLANES = 8
VMEM_LIMIT_BYTES = 56 * 1024 * 1024


def _pick(limit, total, quantum=128):
    if total <= limit:
        return total
    t = (limit // quantum) * quantum
    while t > quantum and total % t:
        t -= quantum
    assert total % t == 0, (limit, total)
    return t


def _cparams(sem):
    return pltpu.CompilerParams(dimension_semantics=sem, vmem_limit_bytes=VMEM_LIMIT_BYTES)


def _allmax8(x):
    for sh in (4, 2, 1):
        x = jnp.maximum(x, pltpu.roll(x, sh, 0))
    return x


def _allsum8(x):
    for sh in (4, 2, 1):
        x = x + pltpu.roll(x, sh, 0)
    return x


def _ln_rows(y, g, b):
    mu = jnp.mean(y, axis=-1, keepdims=True)
    d = y - mu
    var = jnp.mean(d * d, axis=-1, keepdims=True)
    return d * lax.rsqrt(var + LN_EPS) * g + b


def _inproj_kernel(x_ref, w_ref, o_ref, xb_ref, *, transposed):
    @pl.when(pl.program_id(1) == 0)
    def _():
        xb_ref[...] = x_ref[...].astype(BF16)

    if transposed:
        o = lax.dot_general(w_ref[...], xb_ref[...], (((1,), (1,)), ((), ())),
                            preferred_element_type=F32)
    else:
        o = jnp.dot(xb_ref[...], w_ref[...], preferred_element_type=F32)
    o_ref[...] = o.astype(o_ref.dtype)


def _in_proj(x2, w, transposed, tm_lim=1024, tn_lim=1024, name=None):
    m, d = x2.shape
    n = w.shape[0] if transposed else w.shape[1]
    tm = _pick(tm_lim, m)
    tn = _pick(tn_lim, n)
    if transposed:
        w_spec = pl.BlockSpec((tn, d), lambda i, j: (j, 0))
        o_spec = pl.BlockSpec((tn, tm), lambda i, j: (j, i))
        o_shape = jax.ShapeDtypeStruct((n, m), BF16)
    else:
        w_spec = pl.BlockSpec((d, tn), lambda i, j: (0, j))
        o_spec = pl.BlockSpec((tm, tn), lambda i, j: (i, j))
        o_shape = jax.ShapeDtypeStruct((m, n), BF16)
    return pl.pallas_call(
        functools.partial(_inproj_kernel, transposed=transposed),
        grid=(m // tm, n // tn),
        in_specs=[pl.BlockSpec((tm, d), lambda i, j: (i, 0)), w_spec],
        out_specs=o_spec,
        out_shape=o_shape,
        scratch_shapes=[pltpu.VMEM((tm, d), BF16)],
        compiler_params=_cparams(("parallel", "arbitrary")),
        name=name or ("in_proj_t" if transposed else "in_proj"),
    )(x2, w)


LOG2E = math.log2(math.e)
ONES_ROWS = 16


def _t5_bucket_np(dist, num_buckets):
    max_exact = num_buckets // 2
    n_f = np.maximum(dist, 1).astype(np.float32)
    large = max_exact + (np.log(n_f / np.float32(max_exact))
                         / np.float32(math.log(REL_MAX_DISTANCE / max_exact))
                         * np.float32(num_buckets - max_exact)).astype(np.int32)
    large = np.minimum(large, num_buckets - 1)
    return np.where(dist < max_exact, dist, large).astype(np.int32)


def _bias_kernel(rb_ref, bucket_ref, o_ref, *, nb):
    h = pl.program_id(0)
    bucket = bucket_ref[...]
    tile = jnp.zeros(bucket.shape, F32)
    for bkt in range(nb):
        tile = jnp.where(bucket == bkt, rb_ref[bkt, h], tile)
    tile = (tile - rb_ref[nb - 1, h]) * LOG2E
    o_ref[0] = jnp.where(bucket < 0, -jnp.inf, tile)


def _bias_tiles(rel_bias, t):
    nb, n_heads = rel_bias.shape
    assert t + 1 >= REL_MAX_DISTANCE
    key = np.arange(t, dtype=np.int32)[:, None]
    qry = np.arange(t, dtype=np.int32)[None, :]
    d_diag = qry - key
    diag = np.where(d_diag >= 0, _t5_bucket_np(np.maximum(d_diag, 0), nb), -1)
    prev = _t5_bucket_np(d_diag + t, nb)
    buckets = jnp.asarray(np.stack([prev, diag]).astype(np.int32))
    return pl.pallas_call(
        functools.partial(_bias_kernel, nb=nb),
        grid=(n_heads,),
        in_specs=[pl.BlockSpec(memory_space=pltpu.SMEM),
                  pl.BlockSpec((2, t, t), lambda h: (0, 0, 0))],
        out_specs=pl.BlockSpec((1, 2, t, t), lambda h: (h, 0, 0, 0)),
        out_shape=jax.ShapeDtypeStruct((n_heads, 2, t, t), F32),
        compiler_params=_cparams(("arbitrary",)),
        name="rel_bias_tiles",
    )(rel_bias.astype(F32), buckets)


def _attn_kernel(lam_ref, g_ref, qt_ref, k_ref, vt_ref, bias_ref, o_ref, vext_ref, sa_ref, sb_ref,
                    pa_ref, pb_ref, acc_ref, *, t, lambda_init):
    qi = pl.program_id(2)
    dv = ATTN_VALUE_DIM

    @pl.when(qi == 0)
    def _():
        vext_ref[0:dv, :] = vt_ref[...]
        vext_ref[dv:, :] = jnp.ones((ONES_ROWS, vext_ref.shape[1]), BF16)

    qf = qt_ref[...].astype(F32) * (ATTN_HEAD_DIM ** -0.5 * LOG2E)
    row = lax.broadcasted_iota(jnp.int32, qf.shape, 0)
    comps = (jnp.where(row < ATTN_HEAD_DIM, qf, 0.0).astype(BF16),
             jnp.where(row >= ATTN_HEAD_DIM, qf, 0.0).astype(BF16))
    s_bufs = (sa_ref, sb_ref)
    p_bufs = (pa_ref, pb_ref)

    def qk(j, buf):
        kj = k_ref[0, pl.ds(pl.multiple_of(j * t, t), t), :]
        for c in range(2):
            s_bufs[buf][c] = jnp.dot(kj, comps[c], preferred_element_type=F32)

    def pv(j, buf, alphas):
        vj = vext_ref[:, pl.ds(pl.multiple_of(j * t, t), t)]
        for c in range(2):
            acc3 = acc_ref[c].reshape(-1, SUBLANES, t) * alphas[c][None]
            acc_ref[c] = (acc3.reshape(acc_ref.shape[1:])
                          + jnp.dot(vj, p_bufs[buf][c], preferred_element_type=F32))

    def step(j, buf, bias_idx, state, prefetch):
        ms, alphas = state
        if prefetch:
            qk(j + 1, 1 - buf)
        pv(jnp.maximum(j - 1, 0), 1 - buf, alphas)
        s_ref, p_ref = s_bufs[buf], p_bufs[buf]

        def rows(c, r0, n):
            s = s_ref[c, r0:r0 + n, :]
            return s if bias_idx is None else s + bias_ref[0, bias_idx, r0:r0 + n, :]

        out = []
        for c in range(2):
            m8 = rows(c, 0, SUBLANES)
            for r0 in range(SUBLANES, t, SUBLANES):
                m8 = jnp.maximum(m8, rows(c, r0, SUBLANES))
            m_new = jnp.maximum(ms[c], _allmax8(m8))
            m16 = jnp.concatenate([m_new, m_new], axis=0)
            for r0 in range(0, t, 2 * SUBLANES):
                p_ref[c, r0:r0 + 2 * SUBLANES, :] = jnp.exp2(rows(c, r0, 2 * SUBLANES) - m16).astype(BF16)
            out.append((m_new, jnp.exp2(ms[c] - m_new)))
        return tuple(m for m, _ in out), tuple(a for _, a in out)

    acc_ref[...] = jnp.zeros(acc_ref.shape, F32)
    pb_ref[...] = jnp.zeros(pb_ref.shape, BF16)
    m0 = jnp.full((SUBLANES, t), -jnp.inf, F32)
    one = jnp.ones((SUBLANES, t), F32)
    qk(0, 0)
    n_far = jnp.maximum(qi - 1, 0)

    def pair(i, state):
        state = step(2 * i, 0, None, state, True)
        return step(2 * i + 1, 1, None, state, True)

    state = lax.fori_loop(0, n_far // 2, pair, ((m0, m0), (one, one)))

    def tail_first(state):
        _, alphas = step(qi, 0, 1, state, False)
        pv(qi, 0, alphas)
        return 0

    def tail_even(state):
        state = step(qi - 1, 0, 0, state, True)
        _, alphas = step(qi, 1, 1, state, False)
        pv(qi, 1, alphas)
        return 0

    def tail_odd(state):
        state = step(qi - 2, 0, None, state, True)
        state = step(qi - 1, 1, 0, state, True)
        _, alphas = step(qi, 0, 1, state, False)
        pv(qi, 0, alphas)
        return 0

    branch = jnp.where(qi == 0, 0, 1 + n_far % 2)
    lax.switch(branch, [tail_first, tail_even, tail_odd], state)

    lam_p = lam_ref[...]
    lam = (jnp.exp(jnp.sum(lam_p[0:1] * lam_p[1:2], axis=1, keepdims=True))
           - jnp.exp(jnp.sum(lam_p[2:3] * lam_p[3:4], axis=1, keepdims=True)) + lambda_init)
    a1 = acc_ref[0]
    a2 = acc_ref[1]
    o = a1[:dv] / a1[dv:dv + 1] - lam * (a2[:dv] / a2[dv:dv + 1])
    y = o * lax.rsqrt(jnp.mean(o * o, axis=0, keepdims=True) + LN_EPS)
    y = y * (g_ref[...] * (1.0 - lambda_init))
    o_ref[0] = y.T.astype(o_ref.dtype)


def _attention(qvt, k3, lam_params, subln_g_col, bias, n_heads, lambda_init):
    b, s, _ = k3.shape
    t = bias.shape[-1]
    dv = ATTN_VALUE_DIM
    nq = s // t
    kern = functools.partial(_attn_kernel, t=t, lambda_init=lambda_init)
    return pl.pallas_call(
        kern,
        grid=(n_heads, b, nq),
        in_specs=[pl.BlockSpec(lam_params.shape, lambda h, bi, qi: (0, 0)),
                  pl.BlockSpec((dv, 1), lambda h, bi, qi: (0, 0)),
                  pl.BlockSpec((dv, t), lambda h, bi, qi: (h, bi * nq + qi)),
                  pl.BlockSpec((1, s, dv), lambda h, bi, qi: (bi, 0, h)),
                  pl.BlockSpec((dv, s), lambda h, bi, qi: (n_heads + h, bi)),
                  pl.BlockSpec((1, 2, t, t), lambda h, bi, qi: (h, 0, 0, 0))],
        out_specs=pl.BlockSpec((1, t, dv), lambda h, bi, qi: (bi, qi, h)),
        out_shape=jax.ShapeDtypeStruct((b, s, n_heads * dv), BF16),
        scratch_shapes=[pltpu.VMEM((dv + ONES_ROWS, s), BF16),
                        pltpu.VMEM((2, t, t), F32), pltpu.VMEM((2, t, t), F32),
                        pltpu.VMEM((2, t, t), BF16), pltpu.VMEM((2, t, t), BF16),
                        pltpu.VMEM((2, dv + ONES_ROWS, t), F32)],
        compiler_params=_cparams(("parallel", "parallel", "arbitrary")),
        name="diff_attention",
    )(lam_params, subln_g_col, qvt, k3, qvt, bias)


CONV_HALO = 32
CONV_ROW_CHUNK = 64


def _conv_kernel(a_ref, g_ref, ah_ref, gh_ref, bglu_ref, w_ref, cb_ref, lng_ref, lnb_ref, o_ref,
                 h_ref, c_ref, *, ts, kw, ch, rc):
    i = pl.program_id(1)
    ba = bglu_ref[:, :ch]
    bg = bglu_ref[:, ch:]

    def glu(a, g):
        return (a.astype(F32) + ba) * jax.nn.sigmoid(g.astype(F32) + bg)

    halo = glu(ah_ref[0], gh_ref[0])
    h_ref[0, 0:CONV_HALO, :] = jnp.where(i > 0, halo, jnp.zeros_like(halo))
    h_ref[0, CONV_HALO:, :] = glu(a_ref[0], g_ref[0])
    lead = CONV_HALO - (kw - 1)
    span = ts + ((lead + kw - 1) // SUBLANES) * SUBLANES
    for b in range(1, SUBLANES):
        n = min(span, CONV_HALO + ts - b)
        h_ref[b, 0:n, :] = h_ref[0, b:b + n, :]

    for c in range(ch // LANES):
        cs = slice(c * LANES, (c + 1) * LANES)
        wc = w_ref[:, cs]
        cb = cb_ref[:, cs]

        accs = [jnp.zeros((rc // SUBLANES, SUBLANES, LANES), F32) for _ in range(0, ts, rc)]
        for k in range(kw):
            a, b = divmod(lead + k, SUBLANES)
            w8 = jnp.broadcast_to(wc[k:k + 1, :], (SUBLANES, LANES))[None]
            for ri, r0 in enumerate(range(0, ts, rc)):
                off = r0 + a * SUBLANES
                accs[ri] = accs[ri] + h_ref[b, off:off + rc, cs].reshape(accs[ri].shape) * w8
        for ri, r0 in enumerate(range(0, ts, rc)):
            c_ref[r0:r0 + rc, cs] = accs[ri].reshape(rc, LANES) + cb

    y = _ln_rows(c_ref[...], lng_ref[...], lnb_ref[...])
    o_ref[0] = (y * jax.nn.sigmoid(y)).astype(o_ref.dtype)


def _conformer_conv(proj3, b_glu, conv_w, conv_b, ln_g, ln_b, glu_col0, ts_lim=256,
                    rc=CONV_ROW_CHUNK, name="conformer_conv"):
    b, s, _ = proj3.shape
    kw, ch = conv_w.shape
    assert kw - 1 <= CONV_HALO and glu_col0 % ch == 0
    ts = _pick(ts_lim, s)
    cb0 = glu_col0 // ch
    hb = ts // CONV_HALO
    kern = functools.partial(_conv_kernel, ts=ts, kw=kw, ch=ch, rc=min(rc, ts))
    halo_idx = lambda off: (lambda bi, i: (bi, jnp.maximum(i * hb - 1, 0), off))
    vec = lambda n: pl.BlockSpec((1, n), lambda bi, i: (0, 0))
    return pl.pallas_call(
        kern,
        grid=(b, s // ts),
        in_specs=[pl.BlockSpec((1, ts, ch), lambda bi, i: (bi, i, cb0)),
                  pl.BlockSpec((1, ts, ch), lambda bi, i: (bi, i, cb0 + 1)),
                  pl.BlockSpec((1, CONV_HALO, ch), halo_idx(cb0)),
                  pl.BlockSpec((1, CONV_HALO, ch), halo_idx(cb0 + 1)),
                  vec(2 * ch),
                  pl.BlockSpec((kw, ch), lambda bi, i: (0, 0)),
                  vec(ch), vec(ch), vec(ch)],
        out_specs=pl.BlockSpec((1, ts, ch), lambda bi, i: (bi, i, 0)),
        out_shape=jax.ShapeDtypeStruct((b, s, ch), BF16),
        scratch_shapes=[pltpu.VMEM((SUBLANES, CONV_HALO + ts, ch), F32),
                        pltpu.VMEM((ts, ch), F32)],
        compiler_params=_cparams(("parallel", "arbitrary")),
        name=name,
    )(proj3, proj3, proj3, proj3, b_glu, conv_w, conv_b, ln_g, ln_b)


def _outproj_kernel(a_ref, c_ref, wa_ref, wc_ref, x_ref, g_ref, b_ref, o_ref, ob_ref, obt_ref):
    mix = (jnp.dot(a_ref[...], wa_ref[...], preferred_element_type=F32)
           + jnp.dot(c_ref[...], wc_ref[...], preferred_element_type=F32))
    y = _ln_rows(DEEPNORM_ALPHA * x_ref[...] + mix, g_ref[...], b_ref[...])
    o_ref[...] = y
    ob_ref[...] = y.astype(BF16)
    obt_ref[...] = y.T.astype(BF16)


def _out_proj_ln(attn2, conv2, w_out, x2, ln_g, ln_b, tm_lim=512, name="out_proj_ln1"):
    m, d = x2.shape
    da = attn2.shape[1]
    dc = conv2.shape[1]
    tm = _pick(tm_lim, m)
    vec = pl.BlockSpec((1, d), lambda i: (0, 0))
    return pl.pallas_call(
        _outproj_kernel,
        grid=(m // tm,),
        in_specs=[pl.BlockSpec((tm, da), lambda i: (i, 0)),
                  pl.BlockSpec((tm, dc), lambda i: (i, 0)),
                  pl.BlockSpec((da, d), lambda i: (0, 0)),
                  pl.BlockSpec((dc, d), lambda i: (0, 0)),
                  pl.BlockSpec((tm, d), lambda i: (i, 0)),
                  vec, vec],
        out_specs=[pl.BlockSpec((tm, d), lambda i: (i, 0)),
                   pl.BlockSpec((tm, d), lambda i: (i, 0)),
                   pl.BlockSpec((d, tm), lambda i: (0, i))],
        out_shape=[jax.ShapeDtypeStruct((m, d), F32), jax.ShapeDtypeStruct((m, d), BF16),
                   jax.ShapeDtypeStruct((d, m), BF16)],
        compiler_params=_cparams(("parallel",)),
        name=name,
    )(attn2, conv2, w_out[:da], w_out[da:], x2, ln_g, ln_b)


SELECT_UNROLL = 2


def _top_desc(s3, k):
    tops = []
    rank = jnp.full(s3.shape, float(k), F32)
    work = s3
    for a in range(k):
        m = _allmax8(jnp.max(work, axis=0))
        hit = work == m[None]
        rank = jnp.where(hit, float(a), rank)
        work = jnp.where(hit, -jnp.inf, work)
        tops.append(m)
    return tops, rank


def _kth_largest(c3, k):
    work = c3
    m = None
    for _ in range(k):
        m = _allmax8(jnp.max(work, axis=0))
        work = jnp.where(work == m[None], -jnp.inf, work)
    return m


def _rows_of(tops, lo):
    sub = lax.broadcasted_iota(jnp.int32, tops[lo].shape, 0)
    out = tops[lo]
    for r in range(1, SUBLANES):
        out = jnp.where(sub == r, tops[lo + r], out)
    return out


def _peer_select_kernel(x_ref, wq_ref, keys_ref, nea_ref, rb_ref, eb_ref, s_ref,
                        *, n_heads, nk, tt, unroll):
    k = PEER_TOPK
    q = jnp.dot(x_ref[...], wq_ref[...], preferred_element_type=F32).astype(BF16)
    hd = keys_ref.shape[2]
    for hc in range(2 * n_heads):
        s_ref[hc] = lax.dot_general(keys_ref[hc], q[:, hc * hd:(hc + 1) * hd],
                                    (((1,), (1,)), ((), ())), preferred_element_type=F32)

    def group(h, c0):
        cols = pl.ds(c0, LANES)
        g3 = (nk // SUBLANES, SUBLANES, LANES)
        sa = s_ref[2 * h, :, cols].reshape(g3)
        sb = s_ref[2 * h + 1, :, cols].reshape(g3)
        t1, ra = _top_desc(sa, k)
        t2, rb = _top_desc(sb, k)
        s2_lo = _rows_of(t2, 0)
        cand = jnp.stack([t1[0] + s2_lo, t1[0] + _rows_of(t2, SUBLANES)]
                         + [t1[a] + s2_lo for a in range(1, SUBLANES)]
                         + [_rows_of(t1, SUBLANES) + t2[0]])
        tau = _kth_largest(cand, k)
        sel = cand >= tau[None]
        z = _allsum8(jnp.sum(jnp.where(sel, jnp.exp(cand - (t1[0] + t2[0])[None]), 0.0), axis=0))
        picked = jnp.where(sel, 1.0, 0.0)
        n_i = jnp.where((ra < float(k)) & (sa + t2[0][None] >= tau[None]), 1.0, 0.0)
        for a in range(SUBLANES):
            n_a = _allsum8(picked[0] + picked[1] if a == 0 else picked[a + 1])
            n_i = jnp.where(ra == float(a), n_a[None], n_i)
        ea = 0.5 * jnp.exp(sa - t1[0][None]) / z[None]
        nea_ref[h, :, cols] = (2.0 * n_i + ea).reshape(nk, LANES)
        rb_ref[h, :, cols] = rb.reshape(nk, LANES).astype(BF16)
        eb_ref[h, :, cols] = jnp.exp(sb - t2[0][None]).reshape(nk, LANES).astype(BF16)

    per_head = tt // (LANES * unroll)

    def body(it, carry):
        h = it // per_head
        base = (it % per_head) * (LANES * unroll)
        for u in range(unroll):
            group(h, pl.multiple_of(base + u * LANES, LANES))
        return carry

    lax.fori_loop(0, n_heads * per_head, body, 0)


def _peer_select(x1b, wq, keys2, unroll=SELECT_UNROLL, tt_lim=512, name="peer_select"):
    m, d = x1b.shape
    hc, nk, hd = keys2.shape
    n_heads = hc // 2
    assert PEER_TOPK == 2 * SUBLANES and nk % SUBLANES == 0
    tt = _pick(tt_lim, m)
    assert tt % (LANES * unroll) == 0
    kern = functools.partial(_peer_select_kernel, n_heads=n_heads, nk=nk, tt=tt, unroll=unroll)
    out_spec = pl.BlockSpec((n_heads, nk, tt), lambda i: (0, 0, i))
    shp = lambda dt: jax.ShapeDtypeStruct((n_heads, nk, m), dt)
    return pl.pallas_call(
        kern,
        grid=(m // tt,),
        in_specs=[pl.BlockSpec((tt, d), lambda i: (i, 0)),
                  pl.BlockSpec(wq.shape, lambda i: (0, 0)),
                  pl.BlockSpec(keys2.shape, lambda i: (0, 0, 0))],
        out_specs=[out_spec, out_spec, out_spec],
        out_shape=[shp(F32), shp(BF16), shp(BF16)],
        scratch_shapes=[pltpu.VMEM((hc, nk, tt), F32)],
        compiler_params=_cparams(("parallel",)),
        name=name,
    )(x1b, wq, keys2)


PEER_STEP_BLOCKS = 8
PEER_SLAB_BLOCKS = 2
PEER_OUT_PARTS = 1


def _peer_dense_kernel(xt_ref, u_ref, vt_ref, nea_ref, rb_ref, eb_ref, x_ref, g_ref, b_ref,
                       o_ref, acc_ref, *zw_refs, n_heads, ib, nk, slab_blocks, parts):
    c = pl.program_id(1)
    n_slabs = ib // slab_blocks
    z_refs, w_refs = zw_refs[:n_slabs], zw_refs[n_slabs:]

    @pl.when(c == 0)
    def _():
        acc_ref[...] = jnp.zeros(acc_ref.shape, F32)

    slab = slab_blocks * nk
    piece = (ib // parts) * nk
    g0 = pl.multiple_of(c * ib, ib)
    nea = [nea_ref[h, pl.ds(g0, ib), :] for h in range(n_heads)]
    n_rows = [jnp.floor(0.5 * v) for v in nea]
    ea_rows = [v - 2.0 * n for v, n in zip(nea, n_rows)]

    def pre_activation(si):
        z_refs[si][...] = jnp.dot(u_ref[si * slab:(si + 1) * slab, :], xt_ref[...],
                                  preferred_element_type=F32)

    def gate(i):
        si, zi = divmod(i * nk, slab)
        z = z_refs[si][zi:zi + nk, :]
        act = z * (1.0 + lax.erf(z * (2.0 ** -0.5)))
        tt = z.shape[1]
        pack = 2 * SUBLANES
        bshape = (nk // pack, pack, tt)

        def row16(rows):
            return jnp.broadcast_to(rows[i:i + 1, :], (pack, tt)).astype(BF16)[None]

        w = jnp.zeros(bshape, BF16)
        for h in range(n_heads):
            eb = eb_ref[h].reshape(bshape)
            hit = rb_ref[h].reshape(bshape) < row16(n_rows[h])
            w = w + jnp.where(hit, eb, jnp.zeros_like(eb)) * row16(ea_rows[h])
        pi, wi = divmod(i * nk, piece)
        w_refs[pi][wi:wi + nk, :] = act.astype(BF16) * w.reshape(nk, tt)

    pre_activation(0)
    if n_slabs > 1:
        pre_activation(1)
    pv = None
    for i in range(ib):
        si = (i * nk) // slab
        if (i * nk) % slab == 0 and si + 2 < n_slabs:
            pre_activation(si + 2)
        gate(i)
        if (i + 1) * nk % piece == 0:
            pi = (i * nk) // piece
            part = jnp.dot(vt_ref[0, :, pi * piece:(pi + 1) * piece], w_refs[pi][...],
                           preferred_element_type=F32)
            pv = part if pv is None else pv + part
    acc_ref[...] += pv

    @pl.when(c == pl.num_programs(1) - 1)
    def _():
        y = DEEPNORM_ALPHA * x_ref[...] + acc_ref[...].T
        o_ref[...] = _ln_rows(y, g_ref[...], b_ref[...])


def _peer_dense(x1bt, x1, u, v, sel, ln_g, ln_b, nk, ib=PEER_STEP_BLOCKS,
                slab_blocks=PEER_SLAB_BLOCKS, parts=PEER_OUT_PARTS, name="peer_experts_ln2"):
    m, d = x1.shape
    ne = u.shape[0]
    nea, rb, eb = sel
    n_heads = nea.shape[0]
    tt = _pick(512, m)
    ec = ib * nk
    vt = v.reshape(ne // ec, ec, d).transpose(0, 2, 1)
    kern = functools.partial(_peer_dense_kernel, n_heads=n_heads, ib=ib, nk=nk,
                             slab_blocks=slab_blocks, parts=parts)
    once = pl.Buffered(1)
    sel_spec = pl.BlockSpec((n_heads, nk, tt), lambda t, c: (0, 0, t))
    vec = pl.BlockSpec((1, d), lambda t, c: (0, 0))
    return pl.pallas_call(
        kern,
        grid=(m // tt, ne // ec),
        in_specs=[pl.BlockSpec((d, tt), lambda t, c: (0, t)),
                  pl.BlockSpec((ec, d), lambda t, c: (c, 0)),
                  pl.BlockSpec((1, d, ec), lambda t, c: (c, 0, 0)),
                  sel_spec, sel_spec, sel_spec,
                  pl.BlockSpec((tt, d), lambda t, c: (t, 0), pipeline_mode=once),
                  vec, vec],
        out_specs=pl.BlockSpec((tt, d), lambda t, c: (t, 0)),
        out_shape=jax.ShapeDtypeStruct((m, d), F32),
        scratch_shapes=([pltpu.VMEM((d, tt), F32)]
                        + [pltpu.VMEM((slab_blocks * nk, tt), F32)] * (ib // slab_blocks)
                        + [pltpu.VMEM((ec // parts, tt), BF16)] * parts),
        compiler_params=_cparams(("parallel", "arbitrary")),
        name=name,
    )(x1bt, u, vt, nea, rb, eb, x1, ln_g, ln_b)


def kernel(x, w_in, b_glu, conv_w, conv_b, conv_ln_g, conv_ln_b, lambda_q1, lambda_k1, lambda_q2,
           lambda_k2, subln_g, rel_bias, w_out, ln1_g, ln1_b, peer_wq, peer_keys, peer_u, peer_v,
           ln2_g, ln2_b):
    b, s, d = x.shape
    n_layers = w_in.shape[0]
    n_heads = rel_bias.shape[1]
    qk_cols = n_heads * 2 * ATTN_HEAD_DIM
    v_cols = n_heads * ATTN_VALUE_DIM
    t_attn = _pick(512, s)
    bias = _bias_tiles(rel_bias, t_attn)
    row = lambda v: v.reshape(1, -1).astype(F32)

    x2 = x.reshape(b * s, d)
    for layer in range(n_layers):
        lambda_init = 0.8 - 0.6 * math.exp(-0.3 * layer)
        w = w_in[layer].astype(BF16)
        w_q, w_k, w_v, w_glu = (w[:, :qk_cols], w[:, qk_cols:2 * qk_cols],
                                w[:, 2 * qk_cols:2 * qk_cols + v_cols], w[:, 2 * qk_cols + v_cols:])
        kg3 = _in_proj(x2, jnp.concatenate([w_k, w_glu], axis=1), False,
                       tm_lim=1024, tn_lim=1536).reshape(b, s, -1)
        qvt = _in_proj(x2, jnp.concatenate([w_q, w_v], axis=1).T, True, tm_lim=512, tn_lim=2048)
        lam_params = jnp.stack([lambda_q1[layer], lambda_k1[layer],
                                lambda_q2[layer], lambda_k2[layer]]).astype(F32)
        attn = _attention(qvt, kg3, lam_params, subln_g[layer].reshape(-1, 1).astype(F32), bias,
                          n_heads, lambda_init)
        conv = _conformer_conv(kg3, row(b_glu[layer]), conv_w[layer].astype(F32),
                               row(conv_b[layer]), row(conv_ln_g[layer]), row(conv_ln_b[layer]),
                               qk_cols)
        x1, x1b, x1bt = _out_proj_ln(attn.reshape(b * s, -1), conv.reshape(b * s, -1),
                                     w_out[layer].astype(BF16), x2, row(ln1_g[layer]),
                                     row(ln1_b[layer]))
        keys = peer_keys[layer]
        ph, _, nk, hd = keys.shape
        sel = _peer_select(x1b, peer_wq[layer].astype(BF16),
                           keys.reshape(2 * ph, nk, hd).astype(BF16))
        x2 = _peer_dense(x1bt, x1, peer_u[layer].astype(BF16), peer_v[layer].astype(BF16), sel,
                         row(ln2_g[layer]), row(ln2_b[layer]), nk)
    return x2.reshape(b, s, d)
```

```python
import functools
import math

import numpy as np

import jax
import jax.numpy as jnp
from jax import lax
from jax.experimental import pallas as pl
from jax.experimental.pallas import tpu as pltpu

F32 = jnp.float32
BF16 = jnp.bfloat16

ATTN_HEAD_DIM = 64
ATTN_VALUE_DIM = 2 * ATTN_HEAD_DIM
REL_MAX_DISTANCE = 128
PEER_TOPK = 16
LN_EPS = 1e-5
DEPTH = 1
DEEPNORM_ALPHA = (2.0 * DEPTH) ** 0.25

LANES = 128
SUBLANES = 8
VMEM_LIMIT_BYTES = 56 * 1024 * 1024


def _pick(limit, total, quantum=128):
    if total <= limit:
        return total
    t = (limit // quantum) * quantum
    while t > quantum and total % t:
        t -= quantum
    assert total % t == 0, (limit, total)
    return t


def _cparams(sem):
    return pltpu.CompilerParams(dimension_semantics=sem, vmem_limit_bytes=VMEM_LIMIT_BYTES)


def _allmax8(x):
    for sh in (4, 2, 1):
        x = jnp.maximum(x, pltpu.roll(x, sh, 0))
    return x


def _allsum8(x):
    for sh in (4, 2, 1):
        x = x + pltpu.roll(x, sh, 0)
    return x


def _ln_rows(y, g, b):
    mu = jnp.mean(y, axis=-1, keepdims=True)
    d = y - mu
    var = jnp.mean(d * d, axis=-1, keepdims=True)
    return d * lax.rsqrt(var + LN_EPS) * g + b


def _inproj_kernel(x_ref, w_ref, o_ref, xb_ref, *, transposed):
    @pl.when(pl.program_id(1) == 0)
    def _():
        xb_ref[...] = x_ref[...].astype(BF16)

    if transposed:
        o = lax.dot_general(w_ref[...], xb_ref[...], (((1,), (1,)), ((), ())),
                            preferred_element_type=F32)
    else:
        o = jnp.dot(xb_ref[...], w_ref[...], preferred_element_type=F32)
    o_ref[...] = o.astype(o_ref.dtype)


def _in_proj(x2, w, transposed, tm_lim=1024, tn_lim=1024, name=None):
    m, d = x2.shape
    n = w.shape[0] if transposed else w.shape[1]
    tm = _pick(tm_lim, m)
    tn = _pick(tn_lim, n)
    if transposed:
        w_spec = pl.BlockSpec((tn, d), lambda i, j: (j, 0))
        o_spec = pl.BlockSpec((tn, tm), lambda i, j: (j, i))
        o_shape = jax.ShapeDtypeStruct((n, m), BF16)
    else:
        w_spec = pl.BlockSpec((d, tn), lambda i, j: (0, j))
        o_spec = pl.BlockSpec((tm, tn), lambda i, j: (i, j))
        o_shape = jax.ShapeDtypeStruct((m, n), BF16)
    return pl.pallas_call(
        functools.partial(_inproj_kernel, transposed=transposed),
        grid=(m // tm, n // tn),
        in_specs=[pl.BlockSpec((tm, d), lambda i, j: (i, 0)), w_spec],
        out_specs=o_spec,
        out_shape=o_shape,
        scratch_shapes=[pltpu.VMEM((tm, d), BF16)],
        compiler_params=_cparams(("parallel", "arbitrary")),
        name=name or ("in_proj_t" if transposed else "in_proj"),
    )(x2, w)


LOG2E = math.log2(math.e)
ONES_ROWS = 16
ATTN_HEADS_PER_STEP = 1


def _t5_bucket_np(dist, num_buckets):
    max_exact = num_buckets // 2
    n_f = np.maximum(dist, 1).astype(np.float32)
    large = max_exact + (np.log(n_f / np.float32(max_exact))
                         / np.float32(math.log(REL_MAX_DISTANCE / max_exact))
                         * np.float32(num_buckets - max_exact)).astype(np.int32)
    large = np.minimum(large, num_buckets - 1)
    return np.where(dist < max_exact, dist, large).astype(np.int32)


def _bias_kernel(rb_ref, bucket_ref, o_ref, *, nb):
    h = pl.program_id(0)
    bucket = bucket_ref[...]
    tile = jnp.zeros(bucket.shape, F32)
    for bkt in range(nb):
        tile = jnp.where(bucket == bkt, rb_ref[bkt, h], tile)
    tile = (tile - rb_ref[nb - 1, h]) * LOG2E
    o_ref[0] = jnp.where(bucket < 0, -jnp.inf, tile)


def _bias_tiles(rel_bias, t):
    nb, n_heads = rel_bias.shape
    assert t + 1 >= REL_MAX_DISTANCE
    key = np.arange(t, dtype=np.int32)[:, None]
    qry = np.arange(t, dtype=np.int32)[None, :]
    d_diag = qry - key
    diag = np.where(d_diag >= 0, _t5_bucket_np(np.maximum(d_diag, 0), nb), -1)
    prev = _t5_bucket_np(d_diag + t, nb)
    buckets = jnp.asarray(np.stack([prev, diag]).astype(np.int32))
    return pl.pallas_call(
        functools.partial(_bias_kernel, nb=nb),
        grid=(n_heads,),
        in_specs=[pl.BlockSpec(memory_space=pltpu.SMEM),
                  pl.BlockSpec((2, t, t), lambda h: (0, 0, 0))],
        out_specs=pl.BlockSpec((1, 2, t, t), lambda h: (h, 0, 0, 0)),
        out_shape=jax.ShapeDtypeStruct((n_heads, 2, t, t), F32),
        compiler_params=_cparams(("arbitrary",)),
        name="rel_bias_tiles",
    )(rel_bias.astype(F32), buckets)


def _attn_kernel(lam_ref, g_ref, qt_ref, k_ref, vt_ref, bias_ref, o_ref, vext_ref, sa_ref, sb_ref,
                 pa_ref, pb_ref, acc_ref, *, t, nh, lambda_init):
    qi = pl.program_id(2)
    dv = ATTN_VALUE_DIM
    nch = 2 * nh

    @pl.when(qi == 0)
    def _():
        for e in range(nh):
            vext_ref[e, 0:dv, :] = vt_ref[e * dv:(e + 1) * dv, :]
            vext_ref[e, dv:, :] = jnp.ones((ONES_ROWS, vext_ref.shape[2]), BF16)

    comps = []
    for e in range(nh):
        qf = qt_ref[e * dv:(e + 1) * dv, :].astype(F32) * (ATTN_HEAD_DIM ** -0.5 * LOG2E)
        row = lax.broadcasted_iota(jnp.int32, qf.shape, 0)
        comps += [jnp.where(row < ATTN_HEAD_DIM, qf, 0.0).astype(BF16),
                  jnp.where(row >= ATTN_HEAD_DIM, qf, 0.0).astype(BF16)]
    s_bufs = (sa_ref, sb_ref)
    p_bufs = (pa_ref, pb_ref)

    def qk(j, buf):
        kj = k_ref[0, pl.ds(pl.multiple_of(j * t, t), t), :]
        for q in range(nch):
            s_bufs[buf][q] = jnp.dot(kj[:, (q // 2) * dv:(q // 2 + 1) * dv], comps[q],
                                     preferred_element_type=F32)

    def pv(j, buf, alphas):
        cols = pl.ds(pl.multiple_of(j * t, t), t)
        for q in range(nch):
            acc3 = acc_ref[q].reshape(-1, SUBLANES, t) * alphas[q][None]
            acc_ref[q] = (acc3.reshape(acc_ref.shape[1:])
                          + jnp.dot(vext_ref[q // 2, :, cols], p_bufs[buf][q],
                                    preferred_element_type=F32))

    def step(j, buf, bias_idx, state, prefetch):
        ms, alphas = state
        if prefetch:
            qk(j + 1, 1 - buf)
        pv(jnp.maximum(j - 1, 0), 1 - buf, alphas)
        s_ref, p_ref = s_bufs[buf], p_bufs[buf]

        def rows(c, r0, n):
            s = s_ref[c, r0:r0 + n, :]
            return s if bias_idx is None else s + bias_ref[c // 2, bias_idx, r0:r0 + n, :]

        out = []
        for c in range(nch):
            m8 = rows(c, 0, SUBLANES)
            for r0 in range(SUBLANES, t, SUBLANES):
                m8 = jnp.maximum(m8, rows(c, r0, SUBLANES))
            m_new = jnp.maximum(ms[c], _allmax8(m8))
            m16 = jnp.concatenate([m_new, m_new], axis=0)
            for r0 in range(0, t, 2 * SUBLANES):
                p_ref[c, r0:r0 + 2 * SUBLANES, :] = jnp.exp2(rows(c, r0, 2 * SUBLANES) - m16).astype(BF16)
            out.append((m_new, jnp.exp2(ms[c] - m_new)))
        return tuple(m for m, _ in out), tuple(a for _, a in out)

    acc_ref[...] = jnp.zeros(acc_ref.shape, F32)
    pb_ref[...] = jnp.zeros(pb_ref.shape, BF16)
    m0 = jnp.full((SUBLANES, t), -jnp.inf, F32)
    one = jnp.ones((SUBLANES, t), F32)
    qk(0, 0)
    n_far = jnp.maximum(qi - 1, 0)

    def pair(i, state):
        state = step(2 * i, 0, None, state, True)
        return step(2 * i + 1, 1, None, state, True)

    state = lax.fori_loop(0, n_far // 2, pair, ((m0,) * nch, (one,) * nch))

    def tail_first(state):
        _, alphas = step(qi, 0, 1, state, False)
        pv(qi, 0, alphas)
        return 0

    def tail_even(state):
        state = step(qi - 1, 0, 0, state, True)
        _, alphas = step(qi, 1, 1, state, False)
        pv(qi, 1, alphas)
        return 0

    def tail_odd(state):
        state = step(qi - 2, 0, None, state, True)
        state = step(qi - 1, 1, 0, state, True)
        _, alphas = step(qi, 0, 1, state, False)
        pv(qi, 0, alphas)
        return 0

    branch = jnp.where(qi == 0, 0, 1 + n_far % 2)
    lax.switch(branch, [tail_first, tail_even, tail_odd], state)

    lam_p = lam_ref[...]
    lam = (jnp.exp(jnp.sum(lam_p[0:1] * lam_p[1:2], axis=1, keepdims=True))
           - jnp.exp(jnp.sum(lam_p[2:3] * lam_p[3:4], axis=1, keepdims=True)) + lambda_init)
    for e in range(nh):
        a1 = acc_ref[2 * e]
        a2 = acc_ref[2 * e + 1]
        o = a1[:dv] / a1[dv:dv + 1] - lam * (a2[:dv] / a2[dv:dv + 1])
        y = o * lax.rsqrt(jnp.mean(o * o, axis=0, keepdims=True) + LN_EPS)
        y = y * (g_ref[...] * (1.0 - lambda_init))
        o_ref[0, :, e * dv:(e + 1) * dv] = y.T.astype(o_ref.dtype)


def _attention(qvt, k3, lam_params, subln_g_col, bias, n_heads, lambda_init):
    b, s, _ = k3.shape
    t = bias.shape[-1]
    dv = ATTN_VALUE_DIM
    nq = s // t
    nh = ATTN_HEADS_PER_STEP if n_heads % ATTN_HEADS_PER_STEP == 0 else 1
    hd = nh * dv
    kern = functools.partial(_attn_kernel, t=t, nh=nh, lambda_init=lambda_init)
    return pl.pallas_call(
        kern,
        grid=(n_heads // nh, b, nq),
        in_specs=[pl.BlockSpec(lam_params.shape, lambda h, bi, qi: (0, 0)),
                  pl.BlockSpec((dv, 1), lambda h, bi, qi: (0, 0)),
                  pl.BlockSpec((hd, t), lambda h, bi, qi: (h, bi * nq + qi)),
                  pl.BlockSpec((1, s, hd), lambda h, bi, qi: (bi, 0, h)),
                  pl.BlockSpec((hd, s), lambda h, bi, qi: (n_heads // nh + h, bi)),
                  pl.BlockSpec((nh, 2, t, t), lambda h, bi, qi: (h, 0, 0, 0))],
        out_specs=pl.BlockSpec((1, t, hd), lambda h, bi, qi: (bi, qi, h)),
        out_shape=jax.ShapeDtypeStruct((b, s, n_heads * dv), BF16),
        scratch_shapes=[pltpu.VMEM((nh, dv + ONES_ROWS, s), BF16),
                        pltpu.VMEM((2 * nh, t, t), F32), pltpu.VMEM((2 * nh, t, t), F32),
                        pltpu.VMEM((2 * nh, t, t), BF16), pltpu.VMEM((2 * nh, t, t), BF16),
                        pltpu.VMEM((2 * nh, dv + ONES_ROWS, t), F32)],
        compiler_params=_cparams(("parallel", "parallel", "arbitrary")),
        name="diff_attention",
    )(lam_params, subln_g_col, qvt, k3, qvt, bias)


CONV_HALO = 32
CONV_ROW_CHUNK = 64


def _conv_kernel(a_ref, g_ref, ah_ref, gh_ref, bglu_ref, w_ref, cb_ref, lng_ref, lnb_ref, o_ref,
                 h_ref, c_ref, *, ts, kw, ch, rc):
    i = pl.program_id(1)
    ba = bglu_ref[:, :ch]
    bg = bglu_ref[:, ch:]

    def glu(a, g):
        return (a.astype(F32) + ba) * jax.nn.sigmoid(g.astype(F32) + bg)

    halo = glu(ah_ref[0], gh_ref[0])
    h_ref[0, 0:CONV_HALO, :] = jnp.where(i > 0, halo, jnp.zeros_like(halo))
    h_ref[0, CONV_HALO:, :] = glu(a_ref[0], g_ref[0])
    lead = CONV_HALO - (kw - 1)
    span = ts + ((lead + kw - 1) // SUBLANES) * SUBLANES
    for b in range(1, SUBLANES):
        n = min(span, CONV_HALO + ts - b)
        h_ref[b, 0:n, :] = h_ref[0, b:b + n, :]

    for c in range(ch // LANES):
        cs = slice(c * LANES, (c + 1) * LANES)
        wc = w_ref[:, cs]
        cb = cb_ref[:, cs]

        accs = [jnp.zeros((rc // SUBLANES, SUBLANES, LANES), F32) for _ in range(0, ts, rc)]
        for k in range(kw):
            a, b = divmod(lead + k, SUBLANES)
            w8 = jnp.broadcast_to(wc[k:k + 1, :], (SUBLANES, LANES))[None]
            for ri, r0 in enumerate(range(0, ts, rc)):
                off = r0 + a * SUBLANES
                accs[ri] = accs[ri] + h_ref[b, off:off + rc, cs].reshape(accs[ri].shape) * w8
        for ri, r0 in enumerate(range(0, ts, rc)):
            c_ref[r0:r0 + rc, cs] = accs[ri].reshape(rc, LANES) + cb

    y = _ln_rows(c_ref[...], lng_ref[...], lnb_ref[...])
    o_ref[0] = (y * jax.nn.sigmoid(y)).astype(o_ref.dtype)


def _conformer_conv(proj3, b_glu, conv_w, conv_b, ln_g, ln_b, glu_col0, ts_lim=256,
                    rc=CONV_ROW_CHUNK, name="conformer_conv"):
    b, s, _ = proj3.shape
    kw, ch = conv_w.shape
    assert kw - 1 <= CONV_HALO and glu_col0 % ch == 0
    ts = _pick(ts_lim, s)
    cb0 = glu_col0 // ch
    hb = ts // CONV_HALO
    kern = functools.partial(_conv_kernel, ts=ts, kw=kw, ch=ch, rc=min(rc, ts))
    halo_idx = lambda off: (lambda bi, i: (bi, jnp.maximum(i * hb - 1, 0), off))
    vec = lambda n: pl.BlockSpec((1, n), lambda bi, i: (0, 0))
    return pl.pallas_call(
        kern,
        grid=(b, s // ts),
        in_specs=[pl.BlockSpec((1, ts, ch), lambda bi, i: (bi, i, cb0)),
                  pl.BlockSpec((1, ts, ch), lambda bi, i: (bi, i, cb0 + 1)),
                  pl.BlockSpec((1, CONV_HALO, ch), halo_idx(cb0)),
                  pl.BlockSpec((1, CONV_HALO, ch), halo_idx(cb0 + 1)),
                  vec(2 * ch),
                  pl.BlockSpec((kw, ch), lambda bi, i: (0, 0)),
                  vec(ch), vec(ch), vec(ch)],
        out_specs=pl.BlockSpec((1, ts, ch), lambda bi, i: (bi, i, 0)),
        out_shape=jax.ShapeDtypeStruct((b, s, ch), BF16),
        scratch_shapes=[pltpu.VMEM((SUBLANES, CONV_HALO + ts, ch), F32),
                        pltpu.VMEM((ts, ch), F32)],
        compiler_params=_cparams(("parallel", "arbitrary")),
        name=name,
    )(proj3, proj3, proj3, proj3, b_glu, conv_w, conv_b, ln_g, ln_b)


def _outproj_kernel(a_ref, c_ref, wa_ref, wc_ref, x_ref, g_ref, b_ref, o_ref, ob_ref, obt_ref):
    mix = (jnp.dot(a_ref[...], wa_ref[...], preferred_element_type=F32)
           + jnp.dot(c_ref[...], wc_ref[...], preferred_element_type=F32))
    y = _ln_rows(DEEPNORM_ALPHA * x_ref[...] + mix, g_ref[...], b_ref[...])
    o_ref[...] = y
    ob_ref[...] = y.astype(BF16)
    obt_ref[...] = y.T.astype(BF16)


def _out_proj_ln(attn2, conv2, w_out, x2, ln_g, ln_b, tm_lim=512, name="out_proj_ln1"):
    m, d = x2.shape
    da = attn2.shape[1]
    dc = conv2.shape[1]
    tm = _pick(tm_lim, m)
    vec = pl.BlockSpec((1, d), lambda i: (0, 0))
    return pl.pallas_call(
        _outproj_kernel,
        grid=(m // tm,),
        in_specs=[pl.BlockSpec((tm, da), lambda i: (i, 0)),
                  pl.BlockSpec((tm, dc), lambda i: (i, 0)),
                  pl.BlockSpec((da, d), lambda i: (0, 0)),
                  pl.BlockSpec((dc, d), lambda i: (0, 0)),
                  pl.BlockSpec((tm, d), lambda i: (i, 0)),
                  vec, vec],
        out_specs=[pl.BlockSpec((tm, d), lambda i: (i, 0)),
                   pl.BlockSpec((tm, d), lambda i: (i, 0)),
                   pl.BlockSpec((d, tm), lambda i: (0, i))],
        out_shape=[jax.ShapeDtypeStruct((m, d), F32), jax.ShapeDtypeStruct((m, d), BF16),
                   jax.ShapeDtypeStruct((d, m), BF16)],
        compiler_params=_cparams(("parallel",)),
        name=name,
    )(attn2, conv2, w_out[:da], w_out[da:], x2, ln_g, ln_b)


SELECT_UNROLL = 2


def _top_desc(s3, k):
    tops = []
    rank = jnp.full(s3.shape, float(k), F32)
    work = s3
    for a in range(k):
        m = _allmax8(jnp.max(work, axis=0))
        hit = work == m[None]
        rank = jnp.where(hit, float(a), rank)
        work = jnp.where(hit, -jnp.inf, work)
        tops.append(m)
    return tops, rank


def _kth_largest(c3, k):
    work = c3
    m = None
    for _ in range(k):
        m = _allmax8(jnp.max(work, axis=0))
        work = jnp.where(work == m[None], -jnp.inf, work)
    return m


def _rows_of(tops, lo):
    sub = lax.broadcasted_iota(jnp.int32, tops[lo].shape, 0)
    out = tops[lo]
    for r in range(1, SUBLANES):
        out = jnp.where(sub == r, tops[lo + r], out)
    return out


def _peer_select_kernel(x_ref, wq_ref, keys_ref, nea_ref, rb_ref, eb_ref, s_ref,
                        *, n_heads, nk, tt, unroll):
    k = PEER_TOPK
    q = jnp.dot(x_ref[...], wq_ref[...], preferred_element_type=F32).astype(BF16)
    hd = keys_ref.shape[2]
    for hc in range(2 * n_heads):
        s_ref[hc] = lax.dot_general(keys_ref[hc], q[:, hc * hd:(hc + 1) * hd],
                                    (((1,), (1,)), ((), ())), preferred_element_type=F32)

    def group(h, c0):
        cols = pl.ds(c0, LANES)
        g3 = (nk // SUBLANES, SUBLANES, LANES)
        sa = s_ref[2 * h, :, cols].reshape(g3)
        sb = s_ref[2 * h + 1, :, cols].reshape(g3)
        t1, ra = _top_desc(sa, k)
        t2, rb = _top_desc(sb, k)
        s2_lo = _rows_of(t2, 0)
        cand = jnp.stack([t1[0] + s2_lo, t1[0] + _rows_of(t2, SUBLANES)]
                         + [t1[a] + s2_lo for a in range(1, SUBLANES)]
                         + [_rows_of(t1, SUBLANES) + t2[0]])
        tau = _kth_largest(cand, k)
        sel = cand >= tau[None]
        z = _allsum8(jnp.sum(jnp.where(sel, jnp.exp(cand - (t1[0] + t2[0])[None]), 0.0), axis=0))
        picked = jnp.where(sel, 1.0, 0.0)
        n_i = jnp.where((ra < float(k)) & (sa + t2[0][None] >= tau[None]), 1.0, 0.0)
        for a in range(SUBLANES):
            n_a = _allsum8(picked[0] + picked[1] if a == 0 else picked[a + 1])
            n_i = jnp.where(ra == float(a), n_a[None], n_i)
        ea = 0.5 * jnp.exp(sa - t1[0][None]) / z[None]
        nea_ref[h, :, cols] = (2.0 * n_i + ea).reshape(nk, LANES)
        rb_ref[h, :, cols] = rb.reshape(nk, LANES).astype(BF16)
        eb_ref[h, :, cols] = jnp.exp(sb - t2[0][None]).reshape(nk, LANES).astype(BF16)

    per_head = tt // (LANES * unroll)

    def body(it, carry):
        h = it // per_head
        base = (it % per_head) * (LANES * unroll)
        for u in range(unroll):
            group(h, pl.multiple_of(base + u * LANES, LANES))
        return carry

    lax.fori_loop(0, n_heads * per_head, body, 0)


def _peer_select(x1b, wq, keys2, unroll=SELECT_UNROLL, tt_lim=512, name="peer_select"):
    m, d = x1b.shape
    hc, nk, hd = keys2.shape
    n_heads = hc // 2
    assert PEER_TOPK == 2 * SUBLANES and nk % SUBLANES == 0
    tt = _pick(tt_lim, m)
    assert tt % (LANES * unroll) == 0
    kern = functools.partial(_peer_select_kernel, n_heads=n_heads, nk=nk, tt=tt, unroll=unroll)
    out_spec = pl.BlockSpec((n_heads, nk, tt), lambda i: (0, 0, i))
    shp = lambda dt: jax.ShapeDtypeStruct((n_heads, nk, m), dt)
    return pl.pallas_call(
        kern,
        grid=(m // tt,),
        in_specs=[pl.BlockSpec((tt, d), lambda i: (i, 0)),
                  pl.BlockSpec(wq.shape, lambda i: (0, 0)),
                  pl.BlockSpec(keys2.shape, lambda i: (0, 0, 0))],
        out_specs=[out_spec, out_spec, out_spec],
        out_shape=[shp(F32), shp(BF16), shp(BF16)],
        scratch_shapes=[pltpu.VMEM((hc, nk, tt), F32)],
        compiler_params=_cparams(("parallel",)),
        name=name,
    )(x1b, wq, keys2)


PEER_STEP_BLOCKS = 8
PEER_SLAB_BLOCKS = 2
PEER_OUT_PARTS = 1


def _peer_dense_kernel(xt_ref, u_ref, vt_ref, nea_ref, rb_ref, eb_ref, x_ref, g_ref, b_ref,
                       o_ref, acc_ref, *zw_refs, n_heads, ib, nk, slab_blocks, parts):
    c = pl.program_id(1)
    n_slabs = ib // slab_blocks
    z_refs, w_refs = zw_refs[:n_slabs], zw_refs[n_slabs:]

    @pl.when(c == 0)
    def _():
        acc_ref[...] = jnp.zeros(acc_ref.shape, F32)

    slab = slab_blocks * nk
    piece = (ib // parts) * nk
    g0 = pl.multiple_of(c * ib, ib)
    nea = [nea_ref[h, pl.ds(g0, ib), :] for h in range(n_heads)]
    n_rows = [jnp.floor(0.5 * v) for v in nea]
    ea_rows = [v - 2.0 * n for v, n in zip(nea, n_rows)]

    def pre_activation(si):
        z_refs[si][...] = jnp.dot(u_ref[si * slab:(si + 1) * slab, :], xt_ref[...],
                                  preferred_element_type=F32)

    def gate(i):
        si, zi = divmod(i * nk, slab)
        z = z_refs[si][zi:zi + nk, :]
        act = z * (1.0 + lax.erf(z * (2.0 ** -0.5)))
        tt = z.shape[1]
        pack = 2 * SUBLANES
        bshape = (nk // pack, pack, tt)

        def row16(rows):
            return jnp.broadcast_to(rows[i:i + 1, :], (pack, tt)).astype(BF16)[None]

        w = jnp.zeros(bshape, BF16)
        for h in range(n_heads):
            eb = eb_ref[h].reshape(bshape)
            hit = rb_ref[h].reshape(bshape) < row16(n_rows[h])
            w = w + jnp.where(hit, eb, jnp.zeros_like(eb)) * row16(ea_rows[h])
        pi, wi = divmod(i * nk, piece)
        w_refs[pi][wi:wi + nk, :] = act.astype(BF16) * w.reshape(nk, tt)

    pre_activation(0)
    if n_slabs > 1:
        pre_activation(1)
    pv = None
    for i in range(ib):
        si = (i * nk) // slab
        if (i * nk) % slab == 0 and si + 2 < n_slabs:
            pre_activation(si + 2)
        gate(i)
        if (i + 1) * nk % piece == 0:
            pi = (i * nk) // piece
            part = jnp.dot(vt_ref[0, :, pi * piece:(pi + 1) * piece], w_refs[pi][...],
                           preferred_element_type=F32)
            pv = part if pv is None else pv + part
    acc_ref[...] += pv

    @pl.when(c == pl.num_programs(1) - 1)
    def _():
        y = DEEPNORM_ALPHA * x_ref[...] + acc_ref[...].T
        o_ref[...] = _ln_rows(y, g_ref[...], b_ref[...])


def _peer_dense(x1bt, x1, u, v, sel, ln_g, ln_b, nk, ib=PEER_STEP_BLOCKS,
                slab_blocks=PEER_SLAB_BLOCKS, parts=PEER_OUT_PARTS, name="peer_experts_ln2"):
    m, d = x1.shape
    ne = u.shape[0]
    nea, rb, eb = sel
    n_heads = nea.shape[0]
    tt = _pick(512, m)
    ec = ib * nk
    vt = v.reshape(ne // ec, ec, d).transpose(0, 2, 1)
    kern = functools.partial(_peer_dense_kernel, n_heads=n_heads, ib=ib, nk=nk,
                             slab_blocks=slab_blocks, parts=parts)
    once = pl.Buffered(1)
    sel_spec = pl.BlockSpec((n_heads, nk, tt), lambda t, c: (0, 0, t))
    vec = pl.BlockSpec((1, d), lambda t, c: (0, 0))
    return pl.pallas_call(
        kern,
        grid=(m // tt, ne // ec),
        in_specs=[pl.BlockSpec((d, tt), lambda t, c: (0, t)),
                  pl.BlockSpec((ec, d), lambda t, c: (c, 0)),
                  pl.BlockSpec((1, d, ec), lambda t, c: (c, 0, 0)),
                  sel_spec, sel_spec, sel_spec,
                  pl.BlockSpec((tt, d), lambda t, c: (t, 0), pipeline_mode=once),
                  vec, vec],
        out_specs=pl.BlockSpec((tt, d), lambda t, c: (t, 0)),
        out_shape=jax.ShapeDtypeStruct((m, d), F32),
        scratch_shapes=([pltpu.VMEM((d, tt), F32)]
                        + [pltpu.VMEM((slab_blocks * nk, tt), F32)] * (ib // slab_blocks)
                        + [pltpu.VMEM((ec // parts, tt), BF16)] * parts),
        compiler_params=_cparams(("parallel", "arbitrary")),
        name=name,
    )(x1bt, u, vt, nea, rb, eb, x1, ln_g, ln_b)


def kernel(x, w_in, b_glu, conv_w, conv_b, conv_ln_g, conv_ln_b, lambda_q1, lambda_k1, lambda_q2,
           lambda_k2, subln_g, rel_bias, w_out, ln1_g, ln1_b, peer_wq, peer_keys, peer_u, peer_v,
           ln2_g, ln2_b):
    b, s, d = x.shape
    n_layers = w_in.shape[0]
    n_heads = rel_bias.shape[1]
    qk_cols = n_heads * 2 * ATTN_HEAD_DIM
    v_cols = n_heads * ATTN_VALUE_DIM
    t_attn = _pick(512, s)
    bias = _bias_tiles(rel_bias, t_attn)
    row = lambda v: v.reshape(1, -1).astype(F32)

    x2 = x.reshape(b * s, d)
    for layer in range(n_layers):
        lambda_init = 0.8 - 0.6 * math.exp(-0.3 * layer)
        w = w_in[layer].astype(BF16)
        w_q, w_k, w_v, w_glu = (w[:, :qk_cols], w[:, qk_cols:2 * qk_cols],
                                w[:, 2 * qk_cols:2 * qk_cols + v_cols], w[:, 2 * qk_cols + v_cols:])
        kg3 = _in_proj(x2, jnp.concatenate([w_k, w_glu], axis=1), False,
                       tm_lim=1024, tn_lim=1536).reshape(b, s, -1)
        qvt = _in_proj(x2, jnp.concatenate([w_q, w_v], axis=1).T, True, tm_lim=512, tn_lim=2048)
        lam_params = jnp.stack([lambda_q1[layer], lambda_k1[layer],
                                lambda_q2[layer], lambda_k2[layer]]).astype(F32)
        attn = _attention(qvt, kg3, lam_params, subln_g[layer].reshape(-1, 1).astype(F32), bias,
                          n_heads, lambda_init)
        conv = _conformer_conv(kg3, row(b_glu[layer]), conv_w[layer].astype(F32),
                               row(conv_b[layer]), row(conv_ln_g[layer]), row(conv_ln_b[layer]),
                               qk_cols)
        x1, x1b, x1bt = _out_proj_ln(attn.reshape(b * s, -1), conv.reshape(b * s, -1),
                                     w_out[layer].astype(BF16), x2, row(ln1_g[layer]),
                                     row(ln1_b[layer]))
        keys = peer_keys[layer]
        ph, _, nk, hd = keys.shape
        sel = _peer_select(x1b, peer_wq[layer].astype(BF16),
                           keys.reshape(2 * ph, nk, hd).astype(BF16))
        x2 = _peer_dense(x1bt, x1, peer_u[layer].astype(BF16), peer_v[layer].astype(BF16), sel,
                         row(ln2_g[layer]), row(ln2_b[layer]), nk)
    return x2.reshape(b, s, d)
```

```python
import functools
import math

import numpy as np

import jax
import jax.numpy as jnp
from jax import lax
from jax.experimental import pallas as pl
from jax.experimental.pallas import tpu as pltpu

F32 = jnp.float32
BF16 = jnp.bfloat16

ATTN_HEAD_DIM = 64
ATTN_VALUE_DIM = 2 * ATTN_HEAD_DIM
REL_MAX_DISTANCE = 128
PEER_TOPK = 16
LN_EPS = 1e-5
DEPTH = 1
DEEPNORM_ALPHA = (2.0 * DEPTH) ** 0.25

LANES = 128
SUBLANES = 8
VMEM_LIMIT_BYTES = 56 * 1024 * 1024


def _pick(limit, total, quantum=128):
    if total <= limit:
        return total
    t = (limit // quantum) * quantum
    while t > quantum and total % t:
        t -= quantum
    assert total % t == 0, (limit, total)
    return t


def _cparams(sem):
    return pltpu.CompilerParams(dimension_semantics=sem, vmem_limit_bytes=VMEM_LIMIT_BYTES)


def _allmax8(x):
    for sh in (4, 2, 1):
        x = jnp.maximum(x, pltpu.roll(x, sh, 0))
    return x


def _allsum8(x):
    for sh in (4, 2, 1):
        x = x + pltpu.roll(x, sh, 0)
    return x


def _ln_rows(y, g, b):
    mu = jnp.mean(y, axis=-1, keepdims=True)
    d = y - mu
    var = jnp.mean(d * d, axis=-1, keepdims=True)
    return d * lax.rsqrt(var + LN_EPS) * g + b


def _inproj_kernel(x_ref, w_ref, o_ref, xb_ref, *, transposed):
    @pl.when(pl.program_id(1) == 0)
    def _():
        xb_ref[...] = x_ref[...].astype(BF16)

    if transposed:
        o = lax.dot_general(w_ref[...], xb_ref[...], (((1,), (1,)), ((), ())),
                            preferred_element_type=F32)
    else:
        o = jnp.dot(xb_ref[...], w_ref[...], preferred_element_type=F32)
    o_ref[...] = o.astype(o_ref.dtype)


def _in_proj(x2, w, transposed, tm_lim=1024, tn_lim=1024, name=None):
    m, d = x2.shape
    n = w.shape[0] if transposed else w.shape[1]
    tm = _pick(tm_lim, m)
    tn = _pick(tn_lim, n)
    if transposed:
        w_spec = pl.BlockSpec((tn, d), lambda i, j: (j, 0))
        o_spec = pl.BlockSpec((tn, tm), lambda i, j: (j, i))
        o_shape = jax.ShapeDtypeStruct((n, m), BF16)
    else:
        w_spec = pl.BlockSpec((d, tn), lambda i, j: (0, j))
        o_spec = pl.BlockSpec((tm, tn), lambda i, j: (i, j))
        o_shape = jax.ShapeDtypeStruct((m, n), BF16)
    return pl.pallas_call(
        functools.partial(_inproj_kernel, transposed=transposed),
        grid=(m // tm, n // tn),
        in_specs=[pl.BlockSpec((tm, d), lambda i, j: (i, 0)), w_spec],
        out_specs=o_spec,
        out_shape=o_shape,
        scratch_shapes=[pltpu.VMEM((tm, d), BF16)],
        compiler_params=_cparams(("parallel", "arbitrary")),
        name=name or ("in_proj_t" if transposed else "in_proj"),
    )(x2, w)


LOG2E = math.log2(math.e)
ONES_ROWS = 16
ATTN_HEADS_PER_STEP = 1


def _t5_bucket_np(dist, num_buckets):
    max_exact = num_buckets // 2
    n_f = np.maximum(dist, 1).astype(np.float32)
    large = max_exact + (np.log(n_f / np.float32(max_exact))
                         / np.float32(math.log(REL_MAX_DISTANCE / max_exact))
                         * np.float32(num_buckets - max_exact)).astype(np.int32)
    large = np.minimum(large, num_buckets - 1)
    return np.where(dist < max_exact, dist, large).astype(np.int32)


def _bias_kernel(rb_ref, bucket_ref, o_ref, *, nb):
    h = pl.program_id(0)
    bucket = bucket_ref[...]
    tile = jnp.zeros(bucket.shape, F32)
    for bkt in range(nb):
        tile = jnp.where(bucket == bkt, rb_ref[bkt, h], tile)
    tile = (tile - rb_ref[nb - 1, h]) * LOG2E
    o_ref[0] = jnp.where(bucket < 0, -jnp.inf, tile)


def _bias_tiles(rel_bias, t):
    nb, n_heads = rel_bias.shape
    assert t + 1 >= REL_MAX_DISTANCE
    key = np.arange(t, dtype=np.int32)[:, None]
    qry = np.arange(t, dtype=np.int32)[None, :]
    d_diag = qry - key
    diag = np.where(d_diag >= 0, _t5_bucket_np(np.maximum(d_diag, 0), nb), -1)
    prev = _t5_bucket_np(d_diag + t, nb)
    buckets = jnp.asarray(np.stack([prev, diag]).astype(np.int32))
    return pl.pallas_call(
        functools.partial(_bias_kernel, nb=nb),
        grid=(n_heads,),
        in_specs=[pl.BlockSpec(memory_space=pltpu.SMEM),
                  pl.BlockSpec((2, t, t), lambda h: (0, 0, 0))],
        out_specs=pl.BlockSpec((1, 2, t, t), lambda h: (h, 0, 0, 0)),
        out_shape=jax.ShapeDtypeStruct((n_heads, 2, t, t), F32),
        compiler_params=_cparams(("arbitrary",)),
        name="rel_bias_tiles",
    )(rel_bias.astype(F32), buckets)


def _attn_kernel(lam_ref, g_ref, qt_ref, k_ref, vt_ref, bias_ref, o_ref, vext_ref, sa_ref, sb_ref,
                 pa_ref, pb_ref, acc_ref, *, t, nh, lambda_init):
    qi = pl.program_id(2)
    dv = ATTN_VALUE_DIM
    nch = 2 * nh

    @pl.when(qi == 0)
    def _():
        for e in range(nh):
            vext_ref[e, 0:dv, :] = vt_ref[e * dv:(e + 1) * dv, :]
            vext_ref[e, dv:, :] = jnp.ones((ONES_ROWS, vext_ref.shape[2]), BF16)

    comps = []
    for e in range(nh):
        qf = qt_ref[e * dv:(e + 1) * dv, :].astype(F32) * (ATTN_HEAD_DIM ** -0.5 * LOG2E)
        row = lax.broadcasted_iota(jnp.int32, qf.shape, 0)
        comps += [jnp.where(row < ATTN_HEAD_DIM, qf, 0.0).astype(BF16),
                  jnp.where(row >= ATTN_HEAD_DIM, qf, 0.0).astype(BF16)]
    s_bufs = (sa_ref, sb_ref)
    p_bufs = (pa_ref, pb_ref)

    def qk(j, buf):
        kj = k_ref[0, pl.ds(pl.multiple_of(j * t, t), t), :]
        for q in range(nch):
            s_bufs[buf][q] = jnp.dot(kj[:, (q // 2) * dv:(q // 2 + 1) * dv], comps[q],
                                     preferred_element_type=F32)

    def pv(j, buf, alphas):
        cols = pl.ds(pl.multiple_of(j * t, t), t)
        for q in range(nch):
            acc3 = acc_ref[q].reshape(-1, SUBLANES, t) * alphas[q][None]
            acc_ref[q] = (acc3.reshape(acc_ref.shape[1:])
                          + jnp.dot(vext_ref[q // 2, :, cols], p_bufs[buf][q],
                                    preferred_element_type=F32))

    def step(j, buf, bias_idx, state, prefetch):
        ms, alphas = state
        if prefetch:
            qk(j + 1, 1 - buf)
        pv(jnp.maximum(j - 1, 0), 1 - buf, alphas)
        s_ref, p_ref = s_bufs[buf], p_bufs[buf]

        def rows(c, r0, n):
            s = s_ref[c, r0:r0 + n, :]
            return s if bias_idx is None else s + bias_ref[c // 2, bias_idx, r0:r0 + n, :]

        out = []
        for c in range(nch):
            m8 = rows(c, 0, SUBLANES)
            for r0 in range(SUBLANES, t, SUBLANES):
                m8 = jnp.maximum(m8, rows(c, r0, SUBLANES))
            m_new = jnp.maximum(ms[c], _allmax8(m8))
            m16 = jnp.concatenate([m_new, m_new], axis=0)
            for r0 in range(0, t, 2 * SUBLANES):
                p_ref[c, r0:r0 + 2 * SUBLANES, :] = jnp.exp2(rows(c, r0, 2 * SUBLANES) - m16).astype(BF16)
            out.append((m_new, jnp.exp2(ms[c] - m_new)))
        return tuple(m for m, _ in out), tuple(a for _, a in out)

    acc_ref[...] = jnp.zeros(acc_ref.shape, F32)
    pb_ref[...] = jnp.zeros(pb_ref.shape, BF16)
    m0 = jnp.full((SUBLANES, t), -jnp.inf, F32)
    one = jnp.ones((SUBLANES, t), F32)
    qk(0, 0)
    n_far = jnp.maximum(qi - 1, 0)

    def pair(i, state):
        state = step(2 * i, 0, None, state, True)
        return step(2 * i + 1, 1, None, state, True)

    state = lax.fori_loop(0, n_far // 2, pair, ((m0,) * nch, (one,) * nch))

    def tail_first(state):
        _, alphas = step(qi, 0, 1, state, False)
        pv(qi, 0, alphas)
        return 0

    def tail_even(state):
        state = step(qi - 1, 0, 0, state, True)
        _, alphas = step(qi, 1, 1, state, False)
        pv(qi, 1, alphas)
        return 0

    def tail_odd(state):
        state = step(qi - 2, 0, None, state, True)
        state = step(qi - 1, 1, 0, state, True)
        _, alphas = step(qi, 0, 1, state, False)
        pv(qi, 0, alphas)
        return 0

    branch = jnp.where(qi == 0, 0, 1 + n_far % 2)
    lax.switch(branch, [tail_first, tail_even, tail_odd], state)

    lam_p = lam_ref[...]
    lam = (jnp.exp(jnp.sum(lam_p[0:1] * lam_p[1:2], axis=1, keepdims=True))
           - jnp.exp(jnp.sum(lam_p[2:3] * lam_p[3:4], axis=1, keepdims=True)) + lambda_init)
    for e in range(nh):
        a1 = acc_ref[2 * e]
        a2 = acc_ref[2 * e + 1]
        o = a1[:dv] / a1[dv:dv + 1] - lam * (a2[:dv] / a2[dv:dv + 1])
        y = o * lax.rsqrt(jnp.mean(o * o, axis=0, keepdims=True) + LN_EPS)
        y = y * (g_ref[...] * (1.0 - lambda_init))
        o_ref[0, :, e * dv:(e + 1) * dv] = y.T.astype(o_ref.dtype)


def _attention(qvt, k3, lam_params, subln_g_col, bias, n_heads, lambda_init):
    b, s, _ = k3.shape
    t = bias.shape[-1]
    dv = ATTN_VALUE_DIM
    nq = s // t
    nh = ATTN_HEADS_PER_STEP if n_heads % ATTN_HEADS_PER_STEP == 0 else 1
    hd = nh * dv
    kern = functools.partial(_attn_kernel, t=t, nh=nh, lambda_init=lambda_init)
    return pl.pallas_call(
        kern,
        grid=(n_heads // nh, b, nq),
        in_specs=[pl.BlockSpec(lam_params.shape, lambda h, bi, qi: (0, 0)),
                  pl.BlockSpec((dv, 1), lambda h, bi, qi: (0, 0)),
                  pl.BlockSpec((hd, t), lambda h, bi, qi: (h, bi * nq + qi)),
                  pl.BlockSpec((1, s, hd), lambda h, bi, qi: (bi, 0, h)),
                  pl.BlockSpec((hd, s), lambda h, bi, qi: (n_heads // nh + h, bi)),
                  pl.BlockSpec((nh, 2, t, t), lambda h, bi, qi: (h, 0, 0, 0))],
        out_specs=pl.BlockSpec((1, t, hd), lambda h, bi, qi: (bi, qi, h)),
        out_shape=jax.ShapeDtypeStruct((b, s, n_heads * dv), BF16),
        scratch_shapes=[pltpu.VMEM((nh, dv + ONES_ROWS, s), BF16),
                        pltpu.VMEM((2 * nh, t, t), F32), pltpu.VMEM((2 * nh, t, t), F32),
                        pltpu.VMEM((2 * nh, t, t), BF16), pltpu.VMEM((2 * nh, t, t), BF16),
                        pltpu.VMEM((2 * nh, dv + ONES_ROWS, t), F32)],
        compiler_params=_cparams(("parallel", "parallel", "arbitrary")),
        name="diff_attention",
    )(lam_params, subln_g_col, qvt, k3, qvt, bias)


CONV_HALO = 32
CONV_ROW_CHUNK = 64


def _conv_kernel(a_ref, g_ref, ah_ref, gh_ref, bglu_ref, w_ref, cb_ref, lng_ref, lnb_ref, o_ref,
                 h_ref, c_ref, *, ts, kw, ch, rc):
    i = pl.program_id(1)
    ba = bglu_ref[:, :ch]
    bg = bglu_ref[:, ch:]

    def glu(a, g):
        return (a.astype(F32) + ba) * jax.nn.sigmoid(g.astype(F32) + bg)

    halo = glu(ah_ref[0], gh_ref[0])
    h_ref[0, 0:CONV_HALO, :] = jnp.where(i > 0, halo, jnp.zeros_like(halo))
    h_ref[0, CONV_HALO:, :] = glu(a_ref[0], g_ref[0])
    lead = CONV_HALO - (kw - 1)
    span = ts + ((lead + kw - 1) // SUBLANES) * SUBLANES
    for b in range(1, SUBLANES):
        n = min(span, CONV_HALO + ts - b)
        h_ref[b, 0:n, :] = h_ref[0, b:b + n, :]

    for c in range(ch // LANES):
        cs = slice(c * LANES, (c + 1) * LANES)
        wc = w_ref[:, cs]
        cb = cb_ref[:, cs]

        accs = [jnp.zeros((rc // SUBLANES, SUBLANES, LANES), F32) for _ in range(0, ts, rc)]
        for k in range(kw):
            a, b = divmod(lead + k, SUBLANES)
            w8 = jnp.broadcast_to(wc[k:k + 1, :], (SUBLANES, LANES))[None]
            for ri, r0 in enumerate(range(0, ts, rc)):
                off = r0 + a * SUBLANES
                accs[ri] = accs[ri] + h_ref[b, off:off + rc, cs].reshape(accs[ri].shape) * w8
        for ri, r0 in enumerate(range(0, ts, rc)):
            c_ref[r0:r0 + rc, cs] = accs[ri].reshape(rc, LANES) + cb

    y = _ln_rows(c_ref[...], lng_ref[...], lnb_ref[...])
    o_ref[0] = (y * jax.nn.sigmoid(y)).astype(o_ref.dtype)


def _conformer_conv(proj3, b_glu, conv_w, conv_b, ln_g, ln_b, glu_col0, ts_lim=256,
                    rc=CONV_ROW_CHUNK, name="conformer_conv"):
    b, s, _ = proj3.shape
    kw, ch = conv_w.shape
    assert kw - 1 <= CONV_HALO and glu_col0 % ch == 0
    ts = _pick(ts_lim, s)
    cb0 = glu_col0 // ch
    hb = ts // CONV_HALO
    kern = functools.partial(_conv_kernel, ts=ts, kw=kw, ch=ch, rc=min(rc, ts))
    halo_idx = lambda off: (lambda bi, i: (bi, jnp.maximum(i * hb - 1, 0), off))
    vec = lambda n: pl.BlockSpec((1, n), lambda bi, i: (0, 0))
    return pl.pallas_call(
        kern,
        grid=(b, s // ts),
        in_specs=[pl.BlockSpec((1, ts, ch), lambda bi, i: (bi, i, cb0)),
                  pl.BlockSpec((1, ts, ch), lambda bi, i: (bi, i, cb0 + 1)),
                  pl.BlockSpec((1, CONV_HALO, ch), halo_idx(cb0)),
                  pl.BlockSpec((1, CONV_HALO, ch), halo_idx(cb0 + 1)),
                  vec(2 * ch),
                  pl.BlockSpec((kw, ch), lambda bi, i: (0, 0)),
                  vec(ch), vec(ch), vec(ch)],
        out_specs=pl.BlockSpec((1, ts, ch), lambda bi, i: (bi, i, 0)),
        out_shape=jax.ShapeDtypeStruct((b, s, ch), BF16),
        scratch_shapes=[pltpu.VMEM((SUBLANES, CONV_HALO + ts, ch), F32),
                        pltpu.VMEM((ts, ch), F32)],
        compiler_params=_cparams(("parallel", "arbitrary")),
        name=name,
    )(proj3, proj3, proj3, proj3, b_glu, conv_w, conv_b, ln_g, ln_b)


def _outproj_kernel(a_ref, c_ref, wa_ref, wc_ref, x_ref, g_ref, b_ref, o_ref, ob_ref, obt_ref):
    mix = (jnp.dot(a_ref[...], wa_ref[...], preferred_element_type=F32)
           + jnp.dot(c_ref[...], wc_ref[...], preferred_element_type=F32))
    y = _ln_rows(DEEPNORM_ALPHA * x_ref[...] + mix, g_ref[...], b_ref[...])
    o_ref[...] = y
    ob_ref[...] = y.astype(BF16)
    obt_ref[...] = y.T.astype(BF16)


def _out_proj_ln(attn2, conv2, w_out, x2, ln_g, ln_b, tm_lim=512, name="out_proj_ln1"):
    m, d = x2.shape
    da = attn2.shape[1]
    dc = conv2.shape[1]
    tm = _pick(tm_lim, m)
    vec = pl.BlockSpec((1, d), lambda i: (0, 0))
    return pl.pallas_call(
        _outproj_kernel,
        grid=(m // tm,),
        in_specs=[pl.BlockSpec((tm, da), lambda i: (i, 0)),
                  pl.BlockSpec((tm, dc), lambda i: (i, 0)),
                  pl.BlockSpec((da, d), lambda i: (0, 0)),
                  pl.BlockSpec((dc, d), lambda i: (0, 0)),
                  pl.BlockSpec((tm, d), lambda i: (i, 0)),
                  vec, vec],
        out_specs=[pl.BlockSpec((tm, d), lambda i: (i, 0)),
                   pl.BlockSpec((tm, d), lambda i: (i, 0)),
                   pl.BlockSpec((d, tm), lambda i: (0, i))],
        out_shape=[jax.ShapeDtypeStruct((m, d), F32), jax.ShapeDtypeStruct((m, d), BF16),
                   jax.ShapeDtypeStruct((d, m), BF16)],
        compiler_params=_cparams(("parallel",)),
        name=name,
    )(attn2, conv2, w_out[:da], w_out[da:], x2, ln_g, ln_b)


SELECT_UNROLL = 2


def _top_desc(s3, k, ranked):
    tops = []
    rank = jnp.full(s3.shape, float(k), F32)
    work = s3
    for a in range(k):
        m = _allmax8(jnp.max(work, axis=0))
        hit = work == m[None]
        if a < ranked:
            rank = jnp.where(hit, float(a), rank)
        work = jnp.where(hit, -jnp.inf, work)
        tops.append(m)
    return tops, rank, work == -jnp.inf


def _kth_largest(c3, k):
    work = c3
    m = None
    for _ in range(k):
        m = _allmax8(jnp.max(work, axis=0))
        work = jnp.where(work == m[None], -jnp.inf, work)
    return m


def _rows_of(tops, lo):
    sub = lax.broadcasted_iota(jnp.int32, tops[lo].shape, 0)
    out = tops[lo]
    for r in range(1, SUBLANES):
        out = jnp.where(sub == r, tops[lo + r], out)
    return out


def _peer_select_kernel(x_ref, wq_ref, keys_ref, nea_ref, rb_ref, eb_ref, s_ref,
                        *, n_heads, nk, tt, unroll):
    k = PEER_TOPK
    q = jnp.dot(x_ref[...], wq_ref[...], preferred_element_type=F32).astype(BF16)
    hd = keys_ref.shape[2]
    for hc in range(2 * n_heads):
        s_ref[hc] = lax.dot_general(keys_ref[hc], q[:, hc * hd:(hc + 1) * hd],
                                    (((1,), (1,)), ((), ())), preferred_element_type=F32)

    def group(h, c0):
        cols = pl.ds(c0, LANES)
        g3 = (nk // SUBLANES, SUBLANES, LANES)
        sa = s_ref[2 * h, :, cols].reshape(g3)
        sb = s_ref[2 * h + 1, :, cols].reshape(g3)
        t1, ra, taken_a = _top_desc(sa, k, SUBLANES)
        t2, rb, _ = _top_desc(sb, k, k)
        s2_lo = _rows_of(t2, 0)
        cand = jnp.stack([t1[0] + s2_lo, t1[0] + _rows_of(t2, SUBLANES)]
                         + [t1[a] + s2_lo for a in range(1, SUBLANES)]
                         + [_rows_of(t1, SUBLANES) + t2[0]])
        tau = _kth_largest(cand, k)
        sel = cand >= tau[None]
        z = _allsum8(jnp.sum(jnp.where(sel, jnp.exp(cand - (t1[0] + t2[0])[None]), 0.0), axis=0))
        picked = jnp.where(sel, 1.0, 0.0)
        n_i = jnp.where(taken_a & (sa + t2[0][None] >= tau[None]), 1.0, 0.0)
        for a in range(SUBLANES):
            n_a = _allsum8(picked[0] + picked[1] if a == 0 else picked[a + 1])
            n_i = jnp.where(ra == float(a), n_a[None], n_i)
        ea = 0.5 * jnp.exp(sa - t1[0][None]) / z[None]
        nea_ref[h, :, cols] = (2.0 * n_i + ea).reshape(nk, LANES)
        rb_ref[h, :, cols] = rb.reshape(nk, LANES).astype(BF16)
        eb_ref[h, :, cols] = jnp.exp(sb - t2[0][None]).reshape(nk, LANES).astype(BF16)

    per_head = tt // (LANES * unroll)

    def body(it, carry):
        h = it // per_head
        base = (it % per_head) * (LANES * unroll)
        for u in range(unroll):
            group(h, pl.multiple_of(base + u * LANES, LANES))
        return carry

    lax.fori_loop(0, n_heads * per_head, body, 0)


def _peer_select(x1b, wq, keys2, unroll=SELECT_UNROLL, tt_lim=512, name="peer_select"):
    m, d = x1b.shape
    hc, nk, hd = keys2.shape
    n_heads = hc // 2
    assert PEER_TOPK == 2 * SUBLANES and nk % SUBLANES == 0
    tt = _pick(tt_lim, m)
    assert tt % (LANES * unroll) == 0
    kern = functools.partial(_peer_select_kernel, n_heads=n_heads, nk=nk, tt=tt, unroll=unroll)
    out_spec = pl.BlockSpec((n_heads, nk, tt), lambda i: (0, 0, i))
    shp = lambda dt: jax.ShapeDtypeStruct((n_heads, nk, m), dt)
    return pl.pallas_call(
        kern,
        grid=(m // tt,),
        in_specs=[pl.BlockSpec((tt, d), lambda i: (i, 0)),
                  pl.BlockSpec(wq.shape, lambda i: (0, 0)),
                  pl.BlockSpec(keys2.shape, lambda i: (0, 0, 0))],
        out_specs=[out_spec, out_spec, out_spec],
        out_shape=[shp(F32), shp(BF16), shp(BF16)],
        scratch_shapes=[pltpu.VMEM((hc, nk, tt), F32)],
        compiler_params=_cparams(("parallel",)),
        name=name,
    )(x1b, wq, keys2)


PEER_STEP_BLOCKS = 8
PEER_SLAB_BLOCKS = 2
PEER_OUT_PARTS = 1


def _peer_dense_kernel(xt_ref, u_ref, vt_ref, nea_ref, rb_ref, eb_ref, x_ref, g_ref, b_ref,
                       o_ref, acc_ref, *zw_refs, n_heads, ib, nk, slab_blocks, parts):
    c = pl.program_id(1)
    n_slabs = ib // slab_blocks
    z_refs, w_refs = zw_refs[:n_slabs], zw_refs[n_slabs:]

    @pl.when(c == 0)
    def _():
        acc_ref[...] = jnp.zeros(acc_ref.shape, F32)

    slab = slab_blocks * nk
    piece = (ib // parts) * nk
    g0 = pl.multiple_of(c * ib, ib)
    nea = [nea_ref[h, pl.ds(g0, ib), :] for h in range(n_heads)]
    n_rows = [jnp.floor(0.5 * v) for v in nea]
    ea_rows = [v - 2.0 * n for v, n in zip(nea, n_rows)]

    def pre_activation(si):
        z_refs[si][...] = jnp.dot(u_ref[si * slab:(si + 1) * slab, :], xt_ref[...],
                                  preferred_element_type=F32)

    def gate(i):
        si, zi = divmod(i * nk, slab)
        z = z_refs[si][zi:zi + nk, :]
        act = z * (1.0 + lax.erf(z * (2.0 ** -0.5)))
        tt = z.shape[1]
        pack = 2 * SUBLANES
        bshape = (nk // pack, pack, tt)

        def row16(rows):
            return jnp.broadcast_to(rows[i:i + 1, :], (pack, tt)).astype(BF16)[None]

        w = jnp.zeros(bshape, BF16)
        for h in range(n_heads):
            eb = eb_ref[h].reshape(bshape)
            hit = rb_ref[h].reshape(bshape) < row16(n_rows[h])
            w = w + jnp.where(hit, eb, jnp.zeros_like(eb)) * row16(ea_rows[h])
        pi, wi = divmod(i * nk, piece)
        w_refs[pi][wi:wi + nk, :] = act.astype(BF16) * w.reshape(nk, tt)

    pre_activation(0)
    if n_slabs > 1:
        pre_activation(1)
    pv = None
    for i in range(ib):
        si = (i * nk) // slab
        if (i * nk) % slab == 0 and si + 2 < n_slabs:
            pre_activation(si + 2)
        gate(i)
        if (i + 1) * nk % piece == 0:
            pi = (i * nk) // piece
            part = jnp.dot(vt_ref[0, :, pi * piece:(pi + 1) * piece], w_refs[pi][...],
                           preferred_element_type=F32)
            pv = part if pv is None else pv + part
    acc_ref[...] += pv

    @pl.when(c == pl.num_programs(1) - 1)
    def _():
        y = DEEPNORM_ALPHA * x_ref[...] + acc_ref[...].T
        o_ref[...] = _ln_rows(y, g_ref[...], b_ref[...])


def _peer_dense(x1bt, x1, u, v, sel, ln_g, ln_b, nk, ib=PEER_STEP_BLOCKS,
                slab_blocks=PEER_SLAB_BLOCKS, parts=PEER_OUT_PARTS, name="peer_experts_ln2"):
    m, d = x1.shape
    ne = u.shape[0]
    nea, rb, eb = sel
    n_heads = nea.shape[0]
    tt = _pick(512, m)
    ec = ib * nk
    vt = v.reshape(ne // ec, ec, d).transpose(0, 2, 1)
    kern = functools.partial(_peer_dense_kernel, n_heads=n_heads, ib=ib, nk=nk,
                             slab_blocks=slab_blocks, parts=parts)
    once = pl.Buffered(1)
    sel_spec = pl.BlockSpec((n_heads, nk, tt), lambda t, c: (0, 0, t))
    vec = pl.BlockSpec((1, d), lambda t, c: (0, 0))
    return pl.pallas_call(
        kern,
        grid=(m // tt, ne // ec),
        in_specs=[pl.BlockSpec((d, tt), lambda t, c: (0, t)),
                  pl.BlockSpec((ec, d), lambda t, c: (c, 0)),
                  pl.BlockSpec((1, d, ec), lambda t, c: (c, 0, 0)),
                  sel_spec, sel_spec, sel_spec,
                  pl.BlockSpec((tt, d), lambda t, c: (t, 0), pipeline_mode=once),
                  vec, vec],
        out_specs=pl.BlockSpec((tt, d), lambda t, c: (t, 0)),
        out_shape=jax.ShapeDtypeStruct((m, d), F32),
        scratch_shapes=([pltpu.VMEM((d, tt), F32)]
                        + [pltpu.VMEM((slab_blocks * nk, tt), F32)] * (ib // slab_blocks)
                        + [pltpu.VMEM((ec // parts, tt), BF16)] * parts),
        compiler_params=_cparams(("parallel", "arbitrary")),
        name=name,
    )(x1bt, u, vt, nea, rb, eb, x1, ln_g, ln_b)


def kernel(x, w_in, b_glu, conv_w, conv_b, conv_ln_g, conv_ln_b, lambda_q1, lambda_k1, lambda_q2,
           lambda_k2, subln_g, rel_bias, w_out, ln1_g, ln1_b, peer_wq, peer_keys, peer_u, peer_v,
           ln2_g, ln2_b):
    b, s, d = x.shape
    n_layers = w_in.shape[0]
    n_heads = rel_bias.shape[1]
    qk_cols = n_heads * 2 * ATTN_HEAD_DIM
    v_cols = n_heads * ATTN_VALUE_DIM
    t_attn = _pick(512, s)
    bias = _bias_tiles(rel_bias, t_attn)
    row = lambda v: v.reshape(1, -1).astype(F32)

    x2 = x.reshape(b * s, d)
    for layer in range(n_layers):
        lambda_init = 0.8 - 0.6 * math.exp(-0.3 * layer)
        w = w_in[layer].astype(BF16)
        w_q, w_k, w_v, w_glu = (w[:, :qk_cols], w[:, qk_cols:2 * qk_cols],
                                w[:, 2 * qk_cols:2 * qk_cols + v_cols], w[:, 2 * qk_cols + v_cols:])
        kg3 = _in_proj(x2, jnp.concatenate([w_k, w_glu], axis=1), False,
                       tm_lim=1024, tn_lim=1536).reshape(b, s, -1)
        qvt = _in_proj(x2, jnp.concatenate([w_q, w_v], axis=1).T, True, tm_lim=512, tn_lim=2048)
        lam_params = jnp.stack([lambda_q1[layer], lambda_k1[layer],
                                lambda_q2[layer], lambda_k2[layer]]).astype(F32)
        attn = _attention(qvt, kg3, lam_params, subln_g[layer].reshape(-1, 1).astype(F32), bias,
                          n_heads, lambda_init)
        conv = _conformer_conv(kg3, row(b_glu[layer]), conv_w[layer].astype(F32),
                               row(conv_b[layer]), row(conv_ln_g[layer]), row(conv_ln_b[layer]),
                               qk_cols)
        x1, x1b, x1bt = _out_proj_ln(attn.reshape(b * s, -1), conv.reshape(b * s, -1),
                                     w_out[layer].astype(BF16), x2, row(ln1_g[layer]),
                                     row(ln1_b[layer]))
        keys = peer_keys[layer]
        ph, _, nk, hd = keys.shape
        sel = _peer_select(x1b, peer_wq[layer].astype(BF16),
                           keys.reshape(2 * ph, nk, hd).astype(BF16))
        x2 = _peer_dense(x1bt, x1, peer_u[layer].astype(BF16), peer_v[layer].astype(BF16), sel,
                         row(ln2_g[layer]), row(ln2_b[layer]), nk)
    return x2.reshape(b, s, d)
```
